```python
import jax, jax.numpy as jnp
from jax import lax
import numpy as np

D_MODEL = 1024
BATCH = 2
SEQ = 8192
DEPTH = 2

PLE_DIM = 256
EPS = 1e-6
CHUNK = 64
ROPE_THETA = 10000.0
RET_HEADS = 4
RET_DK = 128
RET_DV = 256
RET_QK = RET_HEADS * RET_DK
RET_V = RET_HEADS * RET_DV
GDN_HEADS = 8
GDN_DK = 128
GDN_DV = 128
GDN_QK = GDN_HEADS * GDN_DK
GDN_V = GDN_HEADS * GDN_DV
GDN_CONV_DIM = 2 * GDN_QK + GDN_V
CONV_WIDTH = 4
D_FF = 2816
N_EXPERTS = 8
TOP_K = 2
D_FF_EXPERT = 1408
N_DENSE = (DEPTH + 1) // 2
N_MOE = DEPTH // 2
OFF_RQ = RET_QK
OFF_RK = OFF_RQ + RET_QK
OFF_RV = OFF_RK + RET_V
OFF_RG = OFF_RV + RET_V
OFF_GQKV = OFF_RG + GDN_CONV_DIM
OFF_GZ = OFF_GQKV + GDN_V
OFF_GA = OFF_GZ + GDN_HEADS
OFF_GB = OFF_GA + GDN_HEADS
OFF_MG_RET = OFF_GB + D_MODEL
D_IN = OFF_MG_RET + D_MODEL
SPLIT_IDX = (OFF_RQ, OFF_RK, OFF_RV, OFF_RG, OFF_GQKV, OFF_GZ, OFF_GA, OFF_GB, OFF_MG_RET)

kernel_name = "hybrid_retention_gdn_moe_ple"


def rmsnorm(x, g):
    x32 = x.astype(jnp.float32)
    y = x32 * lax.rsqrt(jnp.mean(x32 * x32, axis=-1, keepdims=True) + EPS)
    return (y * g.astype(jnp.float32)).astype(x.dtype)


def l2norm(x):
    return x * lax.rsqrt(jnp.sum(x * x, axis=-1, keepdims=True) + EPS)


def rotary(x, pos):
    half = x.shape[-1] // 2
    inv_freq = ROPE_THETA ** (-jnp.arange(half, dtype=jnp.float32) / half)
    ang = pos[:, None] * inv_freq[None, :]
    cos = jnp.cos(ang)[None, :, None, :]
    sin = jnp.sin(ang)[None, :, None, :]
    x1, x2 = x[..., :half], x[..., half:]
    return jnp.concatenate([x1 * cos - x2 * sin, x2 * cos + x1 * sin], axis=-1)


def causal_conv(x, w):
    width, ch = w.shape
    return lax.conv_general_dilated(
        x, w[:, None, :].astype(x.dtype), window_strides=(1,), padding=[(width - 1, 0)],
        dimension_numbers=("NWC", "WIO", "NWC"), feature_group_count=ch)


def retention(q, k, v):
    B, T, H, dk = q.shape
    dv = v.shape[-1]
    n = T // CHUNK
    log_gamma = jnp.log1p(-jnp.exp2(-5.0 - jnp.arange(H, dtype=jnp.float32)))
    k = k * (dk ** -0.5)
    qc = q.reshape(B, n, CHUNK, H, dk)
    kc = k.reshape(B, n, CHUNK, H, dk)
    vc = v.reshape(B, n, CHUNK, H, dv)
    idx = jnp.arange(CHUNK, dtype=jnp.float32)
    rel = idx[:, None] - idx[None, :]
    causal = rel >= 0
    dmask = jnp.where(causal[None], jnp.exp(jnp.where(causal, rel, 0.0)[None] * log_gamma[:, None, None]), 0.0)
    scores = jnp.einsum("bnihd,bnjhd->bnhij", qc, kc) * dmask
    intra = jnp.einsum("bnhij,bnjhv->bnihv", scores, vc)
    q_dec = qc * jnp.exp((idx + 1.0)[:, None] * log_gamma[None, :])[:, :, None]
    k_dec = kc * jnp.exp((CHUNK - 1.0 - idx)[:, None] * log_gamma[None, :])[:, :, None]
    chunk_decay = jnp.exp(CHUNK * log_gamma)[None, :, None, None]

    def step(R, xs):
        qd, kd, vv = xs
        inter = jnp.einsum("bchd,bhdv->bchv", qd, R)
        R = R * chunk_decay + jnp.einsum("bchd,bchv->bhdv", kd, vv)
        return R, inter

    R0 = jnp.zeros((B, H, dk, dv), jnp.float32)
    _, inter = lax.scan(step, R0, (q_dec.swapaxes(0, 1), k_dec.swapaxes(0, 1), vc.swapaxes(0, 1)))
    out = intra + inter.swapaxes(0, 1)
    return out.reshape(B, T, H, dv)


def gated_delta_rule(q, k, v, g, beta):
    B, T, H, dk = q.shape
    dv = v.shape[-1]
    n = T // CHUNK
    q = q * (dk ** -0.5)

    def to_chunks(t):
        return t.reshape(B, n, CHUNK, H, -1).transpose(0, 3, 1, 2, 4)

    qc, kc, vc = to_chunks(q), to_chunks(k), to_chunks(v)
    gc = jnp.cumsum(g.reshape(B, n, CHUNK, H).transpose(0, 3, 1, 2), axis=-1)
    bc = beta.reshape(B, n, CHUNK, H).transpose(0, 3, 1, 2)[..., None]
    incl = jnp.tril(jnp.ones((CHUNK, CHUNK), bool))
    strict = jnp.tril(jnp.ones((CHUNK, CHUNK), bool), -1)
    diff = gc[..., :, None] - gc[..., None, :]
    decay_incl = jnp.where(incl, jnp.exp(jnp.where(incl, diff, 0.0)), 0.0)
    kb = kc * bc
    a_mat = jnp.einsum("bhnid,bhnjd->bhnij", kb, kc) * jnp.where(strict, decay_incl, 0.0)
    lhs = jnp.eye(CHUNK, dtype=jnp.float32) + a_mat
    u = lax.linalg.triangular_solve(lhs, vc * bc, left_side=True, lower=True, unit_diagonal=True)
    w = lax.linalg.triangular_solve(lhs, kb * jnp.exp(gc)[..., None], left_side=True, lower=True, unit_diagonal=True)
    qk = jnp.einsum("bhnid,bhnjd->bhnij", qc, kc) * decay_incl
    q_dec = qc * jnp.exp(gc)[..., None]
    g_last = gc[..., -1]
    k_dec = kc * jnp.exp(g_last[..., None] - gc)[..., None]

    def step(S, xs):
        qd, qkn, kd, un, wn, gl = xs
        v_new = un - jnp.einsum("bhcd,bhdv->bhcv", wn, S)
        o = jnp.einsum("bhcd,bhdv->bhcv", qd, S) + jnp.einsum("bhij,bhjv->bhiv", qkn, v_new)
        S = S * jnp.exp(gl)[..., None, None] + jnp.einsum("bhcd,bhcv->bhdv", kd, v_new)
        return S, o

    xs = tuple(jnp.moveaxis(t, 2, 0) for t in (q_dec, qk, k_dec, u, w, g_last))
    S0 = jnp.zeros((B, H, dk, dv), jnp.float32)
    _, o = lax.scan(step, S0, xs)
    return jnp.moveaxis(o, 0, 2).transpose(0, 2, 3, 1, 4).reshape(B, T, H, dv)


def mixer_block(h, g_norm, w_in, conv_w, a_log, dt_bias, ret_gn, gdn_gn, w_br_ret, w_br_gdn, w_out, pos):
    B, T, _ = h.shape
    f32 = jnp.float32
    u = rmsnorm(h, g_norm)
    proj = u @ w_in
    rq, rk, rv, rg, gqkv, gz, ga, gb, mg_ret, mg_gdn = jnp.split(proj, SPLIT_IDX, axis=-1)
    q = rotary(rq.reshape(B, T, RET_HEADS, RET_DK).astype(f32), pos)
    k = rotary(rk.reshape(B, T, RET_HEADS, RET_DK).astype(f32), pos)
    v = rv.reshape(B, T, RET_HEADS, RET_DV).astype(f32)
    o = retention(q, k, v)
    mu = jnp.mean(o, axis=-1, keepdims=True)
    var = jnp.mean(jnp.square(o - mu), axis=-1, keepdims=True)
    o = (o - mu) * lax.rsqrt(var + EPS) * ret_gn.astype(f32).reshape(RET_HEADS, RET_DV)
    y_ret = (o.reshape(B, T, RET_V).astype(h.dtype) * jax.nn.silu(rg)) @ w_br_ret
    qkv = jax.nn.silu(causal_conv(gqkv, conv_w))
    gq, gk, gv = jnp.split(qkv, (GDN_QK, 2 * GDN_QK), axis=-1)
    q = l2norm(gq.reshape(B, T, GDN_HEADS, GDN_DK).astype(f32))
    k = l2norm(gk.reshape(B, T, GDN_HEADS, GDN_DK).astype(f32))
    v = gv.reshape(B, T, GDN_HEADS, GDN_DV).astype(f32)
    beta = jax.nn.sigmoid(gb.astype(f32))
    g = -jnp.exp(a_log.astype(f32)) * jax.nn.softplus(ga.astype(f32) + dt_bias.astype(f32))
    o = gated_delta_rule(q, k, v, g, beta)
    o = o * lax.rsqrt(jnp.mean(o * o, axis=-1, keepdims=True) + EPS) * gdn_gn.astype(f32)
    y_gdn = (o.reshape(B, T, GDN_V).astype(h.dtype) * jax.nn.silu(gz)) @ w_br_gdn
    merged = jax.nn.sigmoid(mg_ret) * y_ret + jax.nn.sigmoid(mg_gdn) * y_gdn
    return h + merged @ w_out


def swiglu(x, wg, wu, wd):
    return (jax.nn.silu(x @ wg) * (x @ wu)) @ wd


def moe(x, router, wg, wu, wd):
    logits = (x @ router).astype(jnp.float32)
    top_logit, top_idx = lax.top_k(logits, TOP_K)
    top_w = jax.nn.softmax(top_logit, axis=-1)
    gates = jnp.sum(jax.nn.one_hot(top_idx, N_EXPERTS, dtype=jnp.float32) * top_w[..., None], axis=-2)
    y = jnp.zeros_like(x)
    for e in range(N_EXPERTS):
        y = y + gates[..., e:e + 1].astype(x.dtype) * swiglu(x, wg[e], wu[e], wd[e])
    return y


def setup_inputs(seed: int = 0) -> dict:
    key = jax.random.key(seed)
    ks = jax.random.split(key, 26)
    f32 = jnp.float32

    def nrm(k, shape, fan_in):
        return jax.random.normal(k, shape, f32) * (fan_in ** -0.5)

    def gain(k, shape):
        return 1.0 + 0.02 * jax.random.normal(k, shape, f32)

    dt = jnp.exp(jax.random.uniform(ks[5], (DEPTH, GDN_HEADS), f32, np.log(1e-3), np.log(1e-1)))
    return {
        "x": jax.random.normal(ks[0], (BATCH, SEQ, D_MODEL), f32),
        "p": jax.random.normal(ks[1], (DEPTH, BATCH, SEQ, PLE_DIM), f32),
        "w_in": nrm(ks[2], (DEPTH, D_MODEL, D_IN), D_MODEL),
        "conv_w": 0.5 * jax.random.normal(ks[3], (DEPTH, CONV_WIDTH, GDN_CONV_DIM), f32),
        "a_log": jnp.log(jax.random.uniform(ks[4], (DEPTH, GDN_HEADS), f32, 1.0, 16.0)),
        "dt_bias": dt + jnp.log(-jnp.expm1(-dt)),
        "ret_gn": gain(ks[6], (DEPTH, RET_V)),
        "gdn_gn": gain(ks[7], (DEPTH, GDN_DV)),
        "w_br_ret": nrm(ks[8], (DEPTH, RET_V, D_MODEL), RET_V),
        "w_br_gdn": nrm(ks[9], (DEPTH, GDN_V, D_MODEL), GDN_V),
        "w_out": nrm(ks[10], (DEPTH, D_MODEL, D_MODEL), D_MODEL),
        "norm_mix": gain(ks[11], (DEPTH, D_MODEL)),
        "norm_ffn": gain(ks[12], (DEPTH, D_MODEL)),
        "norm_ple": gain(ks[13], (DEPTH, D_MODEL)),
        "w_ple": nrm(ks[14], (DEPTH, PLE_DIM, D_MODEL), PLE_DIM),
        "w_ple_gate": nrm(ks[15], (DEPTH, D_MODEL, D_MODEL), D_MODEL),
        "ffn_w_gate": nrm(ks[16], (N_DENSE, D_MODEL, D_FF), D_MODEL),
        "ffn_w_up": nrm(ks[17], (N_DENSE, D_MODEL, D_FF), D_MODEL),
        "ffn_w_down": nrm(ks[18], (N_DENSE, D_FF, D_MODEL), D_FF),
        "router": nrm(ks[19], (N_MOE, D_MODEL, N_EXPERTS), D_MODEL),
        "exp_w_gate": nrm(ks[20], (N_MOE, N_EXPERTS, D_MODEL, D_FF_EXPERT), D_MODEL),
        "exp_w_up": nrm(ks[21], (N_MOE, N_EXPERTS, D_MODEL, D_FF_EXPERT), D_MODEL),
        "exp_w_down": nrm(ks[22], (N_MOE, N_EXPERTS, D_FF_EXPERT, D_MODEL), D_FF_EXPERT),
        "norm_final": gain(ks[23], (D_MODEL,)),
    }


def reference(x, p, w_in, conv_w, a_log, dt_bias, ret_gn, gdn_gn, w_br_ret, w_br_gdn, w_out,
              norm_mix, norm_ffn, norm_ple, w_ple, w_ple_gate, ffn_w_gate, ffn_w_up, ffn_w_down,
              router, exp_w_gate, exp_w_up, exp_w_down, norm_final):
    pos = jnp.arange(x.shape[1], dtype=jnp.float32)
    h = x
    for i in range(DEPTH):
        h = mixer_block(h, norm_mix[i], w_in[i], conv_w[i], a_log[i], dt_bias[i], ret_gn[i], gdn_gn[i],
                        w_br_ret[i], w_br_gdn[i], w_out[i], pos)
        u = rmsnorm(h, norm_ffn[i])
        j = i // 2
        if i % 2 == 0:
            h = h + swiglu(u, ffn_w_gate[j], ffn_w_up[j], ffn_w_down[j])
        else:
            h = h + moe(u, router[j], exp_w_gate[j], exp_w_up[j], exp_w_down[j])
        ple = p[i] @ w_ple[i]
        h = h + jax.nn.sigmoid(rmsnorm(h, norm_ple[i]) @ w_ple_gate[i]) * ple
    return rmsnorm(h, norm_final)
```

```python
import functools

import numpy as np
import jax
import jax.numpy as jnp
from jax import lax
from jax.experimental import pallas as pl
from jax.experimental.pallas import tpu as pltpu

F32 = jnp.float32
BF16 = jnp.bfloat16

D_MODEL = 1024
PLE_DIM = 256
EPS = 1e-6
ROPE_THETA = 10000.0
RET_HEADS = 4
RET_DK = 128
RET_DV = 256
RET_QK = RET_HEADS * RET_DK
RET_V = RET_HEADS * RET_DV
GDN_HEADS = 8
GDN_DK = 128
GDN_DV = 128
GDN_QK = GDN_HEADS * GDN_DK
GDN_V = GDN_HEADS * GDN_DV
GDN_CONV_DIM = 2 * GDN_QK + GDN_V
CONV_WIDTH = 4
D_FF = 2816
N_EXPERTS = 8
D_FF_EXPERT = 1408
OFF_GA = 2 * RET_QK + 2 * RET_V + GDN_CONV_DIM + GDN_V
OFF_MG = OFF_GA + 2 * GDN_HEADS
N_MAIN = OFF_GA + 2 * D_MODEL
LANES = 128
GDN_CHUNK = 64
RET_CHUNK = 256
HALO = 8
VMEM_LIMIT = 56 * 1024 * 1024

_BLK_RV, _BLK_RG, _BLK_GQ, _BLK_GK, _BLK_GV, _BLK_GZ, _BLK_MGR, _BLK_MGG = 1, 2, 3, 4, 5, 6, 7, 8


def _silu(x):
    return x * jax.nn.sigmoid(x)


def _rms(x, g):
    return x * lax.rsqrt(jnp.mean(x * x, axis=-1, keepdims=True) + EPS) * g


def _dot(a, b):
    return jnp.dot(a, b, preferred_element_type=F32)


def _dot_nt(a, b):
    return lax.dot_general(a, b, (((1,), (1,)), ((), ())), preferred_element_type=F32)


def _params(*sem):
    return pltpu.CompilerParams(dimension_semantics=sem, vmem_limit_bytes=VMEM_LIMIT)


def _in_proj_kernel(x_ref, g_ref, wm_ref, wg_ref, om_ref, og_ref, u_ref, *, n_main):
    j = pl.program_id(1)

    @pl.when(j == 0)
    def _():
        u_ref[...] = _rms(x_ref[...], g_ref[...]).astype(BF16)

    @pl.when(j < n_main)
    def _():
        om_ref[...] = _dot(u_ref[...], wm_ref[...]).astype(BF16)

    @pl.when(j == n_main)
    def _():
        og_ref[...] = _dot(u_ref[...], wg_ref[...])


def _in_proj(h, g, w_main, w_gab, tm=1024, tn=1024):
    n = h.shape[0]
    n_main = N_MAIN // tn
    last = n_main - 1
    return pl.pallas_call(
        functools.partial(_in_proj_kernel, n_main=n_main),
        grid=(n // tm, n_main + 1),
        in_specs=[
            pl.BlockSpec((tm, D_MODEL), lambda i, j: (i, 0)),
            pl.BlockSpec((1, D_MODEL), lambda i, j: (0, 0)),
            pl.BlockSpec((D_MODEL, tn), lambda i, j: (0, jnp.minimum(j, last))),
            pl.BlockSpec((D_MODEL, LANES), lambda i, j: (0, 0)),
        ],
        out_specs=[
            pl.BlockSpec((tm, tn), lambda i, j: (i, jnp.minimum(j, last))),
            pl.BlockSpec((tm, LANES), lambda i, j: (i, 0)),
        ],
        out_shape=[
            jax.ShapeDtypeStruct((n, N_MAIN), BF16),
            jax.ShapeDtypeStruct((n, LANES), F32),
        ],
        scratch_shapes=[pltpu.VMEM((tm, D_MODEL), BF16)],
        compiler_params=_params("parallel", "arbitrary"),
        name="in_proj",
    )(h, g, w_main, w_gab)


def _ret_kernel(q_ref, k_ref, v_ref, rg_ref, cos_ref, sin_ref, dmask_ref, qdec_ref, kdec_ref,
                gn_ref, o_ref, state_ref, *, chunk_decay):
    @pl.when(pl.program_id(1) == 0)
    def _():
        state_ref[...] = jnp.zeros_like(state_ref)

    cos = cos_ref[...]
    sin = sin_ref[...]
    for hd in range(RET_HEADS):
        qk_sl = slice(hd * RET_DK, (hd + 1) * RET_DK)
        v_sl = slice(hd * RET_DV, (hd + 1) * RET_DV)
        q = q_ref[:, qk_sl].astype(F32)
        k = k_ref[:, qk_sl].astype(F32)
        q = q * cos + pltpu.roll(q, RET_DK // 2, 1) * sin
        k = (k * cos + pltpu.roll(k, RET_DK // 2, 1) * sin) * (RET_DK ** -0.5)
        v = v_ref[:, v_sl]
        scores = _dot_nt(q.astype(BF16), k.astype(BF16)) * dmask_ref[hd]
        state = state_ref[hd]
        o = _dot(scores.astype(BF16), v) + _dot((q * qdec_ref[:, qk_sl]).astype(BF16), state.astype(BF16))
        k_dec_t = (k * kdec_ref[:, qk_sl]).T.astype(BF16)
        state_ref[hd] = state * chunk_decay[hd] + _dot(k_dec_t, v)
        mu = jnp.mean(o, axis=-1, keepdims=True)
        oc = o - mu
        var = jnp.mean(oc * oc, axis=-1, keepdims=True)
        o = oc * lax.rsqrt(var + EPS) * gn_ref[:, v_sl]
        o_ref[:, v_sl] = (o * _silu(rg_ref[:, v_sl].astype(F32))).astype(BF16)


def _retention(proj, cos_t, sin_t, ret_gn, batch, seq):
    c = RET_CHUNK
    nc = seq // c
    log_gamma = np.log1p(-np.exp2(-5.0 - np.arange(RET_HEADS, dtype=np.float64)))
    chunk_decay = tuple(float(np.exp(c * lg)) for lg in log_gamma)
    lg = jnp.asarray(log_gamma, F32)
    idx = jnp.arange(c, dtype=F32)
    rel = idx[:, None] - idx[None, :]
    causal = rel >= 0
    dmask = jnp.where(causal[None], jnp.exp(jnp.where(causal, rel, 0.0)[None] * lg[:, None, None]), 0.0)
    qdec = jnp.repeat(jnp.exp((idx + 1.0)[:, None] * lg[None, :]), RET_DK, axis=1)
    kdec = jnp.repeat(jnp.exp((c - 1.0 - idx)[:, None] * lg[None, :]), RET_DK, axis=1)
    row = lambda b, t: b * nc + t
    return pl.pallas_call(
        functools.partial(_ret_kernel, chunk_decay=chunk_decay),
        grid=(batch, nc),
        in_specs=[
            pl.BlockSpec((c, RET_QK), lambda b, t: (row(b, t), 0)),
            pl.BlockSpec((c, RET_QK), lambda b, t: (row(b, t), 1)),
            pl.BlockSpec((c, RET_V), lambda b, t: (row(b, t), _BLK_RV)),
            pl.BlockSpec((c, RET_V), lambda b, t: (row(b, t), _BLK_RG)),
            pl.BlockSpec((c, RET_DK), lambda b, t: (t, 0)),
            pl.BlockSpec((c, RET_DK), lambda b, t: (t, 0)),
            pl.BlockSpec((RET_HEADS, c, c), lambda b, t: (0, 0, 0)),
            pl.BlockSpec((c, RET_QK), lambda b, t: (0, 0)),
            pl.BlockSpec((c, RET_QK), lambda b, t: (0, 0)),
            pl.BlockSpec((1, RET_V), lambda b, t: (0, 0)),
        ],
        out_specs=pl.BlockSpec((c, RET_V), lambda b, t: (row(b, t), 0)),
        out_shape=jax.ShapeDtypeStruct((batch * seq, RET_V), BF16),
        scratch_shapes=[pltpu.VMEM((RET_HEADS, RET_DK, RET_DV), F32)],
        compiler_params=_params("parallel", "arbitrary"),
        name="retention",
    )(proj, proj, proj, proj, cos_t, sin_t, dmask, qdec, kdec, ret_gn)


def _gdn_kernel(gq_ref, gk_ref, gv_ref, gz_ref, gab_ref, cw_ref, alog_ref, dtb_ref, gn_ref,
                o_ref, state_ref, buf_ref):
    c = GDN_CHUNK

    @pl.when(pl.program_id(1) == 0)
    def _():
        state_ref[...] = jnp.zeros_like(state_ref)
        buf_ref[0:HALO, :] = jnp.zeros((HALO, GDN_CONV_DIM), F32)

    buf_ref[HALO:HALO + c, 0:GDN_QK] = gq_ref[...].astype(F32)
    buf_ref[HALO:HALO + c, GDN_QK:2 * GDN_QK] = gk_ref[...].astype(F32)
    buf_ref[HALO:HALO + c, 2 * GDN_QK:GDN_CONV_DIM] = gv_ref[...].astype(F32)
    y = buf_ref[HALO:HALO + c, :] * cw_ref[CONV_WIDTH - 1:CONV_WIDTH, :]
    for s in range(1, CONV_WIDTH):
        y = y + buf_ref[HALO - s:HALO - s + c, :] * cw_ref[CONV_WIDTH - 1 - s:CONV_WIDTH - s, :]
    buf_ref[0:HALO, :] = buf_ref[c:c + HALO, :]
    y = _silu(y)

    gab = gab_ref[...]
    z = gab + dtb_ref[...]
    softplus = jnp.maximum(z, 0.0) + jnp.log(1.0 + jnp.exp(-jnp.abs(z)))
    g = -jnp.exp(alog_ref[...]) * softplus
    beta = jax.nn.sigmoid(gab)
    row = lax.broadcasted_iota(jnp.int32, (c, LANES), 0)
    gc = g
    for s in (1, 2, 4, 8, 16, 32):
        gc = gc + jnp.where(row >= s, pltpu.roll(gc, s, 0), 0.0)
    gc_t = gc.T

    ii = lax.broadcasted_iota(jnp.int32, (c, c), 0)
    jj = lax.broadcasted_iota(jnp.int32, (c, c), 1)
    incl = ii >= jj
    strict = ii > jj
    gn = gn_ref[...]
    for hd in range(GDN_HEADS):
        sl = slice(hd * GDN_DK, (hd + 1) * GDN_DK)
        q = y[:, sl]
        k = y[:, GDN_QK + hd * GDN_DK:GDN_QK + (hd + 1) * GDN_DK]
        v = y[:, 2 * GDN_QK + hd * GDN_DV:2 * GDN_QK + (hd + 1) * GDN_DV]
        q = q * lax.rsqrt(jnp.sum(q * q, axis=-1, keepdims=True) + EPS) * (GDN_DK ** -0.5)
        k = k * lax.rsqrt(jnp.sum(k * k, axis=-1, keepdims=True) + EPS)
        g_col = gc[:, hd:hd + 1]
        g_row = gc_t[hd:hd + 1, :]
        g_last = gc[c - 1:c, hd:hd + 1]
        b_col = beta[:, GDN_HEADS + hd:GDN_HEADS + hd + 1]
        decay = jnp.where(incl, jnp.exp(jnp.where(incl, g_col - g_row, 0.0)), 0.0)
        kb = k * b_col
        k_bf = k.astype(BF16)
        both = _dot_nt(jnp.concatenate([kb, q], axis=0).astype(BF16), k_bf)
        a_mat = both[0:c] * jnp.where(strict, decay, 0.0)
        qk = both[c:2 * c] * decay
        e_col = jnp.exp(g_col)
        x = jnp.concatenate([v * b_col, kb * e_col], axis=1)
        m = -a_mat
        for it in range(6):
            m_bf = m.astype(BF16)
            x = x + _dot(m_bf, x.astype(BF16))
            if it < 5:
                m = _dot(m_bf, m_bf)
        u = x[:, 0:GDN_DV]
        w = x[:, GDN_DV:2 * GDN_DV]
        state = state_ref[hd]
        ws = _dot(jnp.concatenate([w, q * e_col], axis=0).astype(BF16), state.astype(BF16))
        v_new = u - ws[0:c]
        v_new_bf = v_new.astype(BF16)
        o = ws[c:2 * c] + _dot(qk.astype(BF16), v_new_bf)
        k_dec_t = (k * jnp.exp(g_last - g_col)).T.astype(BF16)
        state_ref[hd] = state * jnp.exp(g_last) + _dot(k_dec_t, v_new_bf)
        o = o * lax.rsqrt(jnp.mean(o * o, axis=-1, keepdims=True) + EPS) * gn
        o_ref[:, sl] = (o * _silu(gz_ref[:, sl].astype(F32))).astype(BF16)


def _gdn(proj, gab, conv_w, a_log, dt_bias, gdn_gn, batch, seq):
    c = GDN_CHUNK
    nc = seq // c
    row = lambda b, t: b * nc + t
    pad = lambda a: jnp.pad(a.reshape(1, -1), ((0, 0), (0, LANES - a.shape[-1])))
    return pl.pallas_call(
        _gdn_kernel,
        grid=(batch, nc),
        in_specs=[
            pl.BlockSpec((c, GDN_QK), lambda b, t: (row(b, t), _BLK_GQ)),
            pl.BlockSpec((c, GDN_QK), lambda b, t: (row(b, t), _BLK_GK)),
            pl.BlockSpec((c, GDN_V), lambda b, t: (row(b, t), _BLK_GV)),
            pl.BlockSpec((c, GDN_V), lambda b, t: (row(b, t), _BLK_GZ)),
            pl.BlockSpec((c, LANES), lambda b, t: (row(b, t), 0)),
            pl.BlockSpec((CONV_WIDTH, GDN_CONV_DIM), lambda b, t: (0, 0)),
            pl.BlockSpec((1, LANES), lambda b, t: (0, 0)),
            pl.BlockSpec((1, LANES), lambda b, t: (0, 0)),
            pl.BlockSpec((1, GDN_DV), lambda b, t: (0, 0)),
        ],
        out_specs=pl.BlockSpec((c, GDN_V), lambda b, t: (row(b, t), 0)),
        out_shape=jax.ShapeDtypeStruct((batch * seq, GDN_V), BF16),
        scratch_shapes=[
            pltpu.VMEM((GDN_HEADS, GDN_DK, GDN_DV), F32),
            pltpu.VMEM((HALO + c, GDN_CONV_DIM), F32),
        ],
        compiler_params=_params("parallel", "arbitrary"),
        name="gdn",
    )(proj, proj, proj, proj, gab, conv_w, pad(a_log), pad(dt_bias), gdn_gn.reshape(1, -1))


def _mix_out_kernel(h_ref, oret_ref, ogdn_ref, mgr_ref, mgg_ref, wbr_ref, wbg_ref, wo_ref, out_ref):
    y_ret = _dot(oret_ref[...], wbr_ref[...])
    y_gdn = _dot(ogdn_ref[...], wbg_ref[...])
    merged = (jax.nn.sigmoid(mgr_ref[...].astype(F32)) * y_ret
              + jax.nn.sigmoid(mgg_ref[...].astype(F32)) * y_gdn)
    out_ref[...] = h_ref[...] + _dot(merged.astype(BF16), wo_ref[...])


def _mix_out(h, o_ret, o_gdn, proj, w_br_ret, w_br_gdn, w_out, tm=512):
    n = h.shape[0]
    tile = lambda blk: pl.BlockSpec((tm, D_MODEL), lambda i: (i, blk))
    weight = pl.BlockSpec((D_MODEL, D_MODEL), lambda i: (0, 0))
    return pl.pallas_call(
        _mix_out_kernel,
        grid=(n // tm,),
        in_specs=[tile(0), tile(0), tile(0), tile(_BLK_MGR), tile(_BLK_MGG), weight, weight, weight],
        out_specs=tile(0),
        out_shape=jax.ShapeDtypeStruct((n, D_MODEL), F32),
        compiler_params=_params("parallel"),
        name="mix_out",
    )(h, o_ret, o_gdn, proj, proj, w_br_ret, w_br_gdn, w_out)


def _ple_tail(h1, p_ref, gple_ref, wple_ref, wpg_ref, gfin_ref, out_ref, final):
    ple = _dot(p_ref[...].astype(BF16), wple_ref[...])
    gate = jax.nn.sigmoid(_dot(_rms(h1, gple_ref[...]).astype(BF16), wpg_ref[...]))
    h2 = h1 + gate * ple
    out_ref[...] = _rms(h2, gfin_ref[...]) if final else h2


def _ffn_kernel(h_ref, g_ref, wg_ref, wu_ref, wd_ref, p_ref, gple_ref, wple_ref, wpg_ref, gfin_ref,
                out_ref, u_ref, acc_ref, *, final):
    j = pl.program_id(1)

    @pl.when(j == 0)
    def _():
        u_ref[...] = _rms(h_ref[...], g_ref[...]).astype(BF16)
        acc_ref[...] = jnp.zeros_like(acc_ref)

    u = u_ref[...]
    act = _silu(_dot(u, wg_ref[...])) * _dot(u, wu_ref[...])
    acc_ref[...] += _dot(act.astype(BF16), wd_ref[...])

    @pl.when(j == pl.num_programs(1) - 1)
    def _():
        _ple_tail(h_ref[...] + acc_ref[...], p_ref, gple_ref, wple_ref, wpg_ref, gfin_ref, out_ref, final)


def _ffn(h, g, wg, wu, wd, p, g_ple, w_ple, w_ple_gate, g_final, final, tm=1024, tf=256):
    n = h.shape[0]
    const = lambda shape: pl.BlockSpec(shape, lambda i, j: (0, 0))
    return pl.pallas_call(
        functools.partial(_ffn_kernel, final=final),
        grid=(n // tm, D_FF // tf),
        in_specs=[
            pl.BlockSpec((tm, D_MODEL), lambda i, j: (i, 0)),
            const((1, D_MODEL)),
            pl.BlockSpec((D_MODEL, tf), lambda i, j: (0, j)),
            pl.BlockSpec((D_MODEL, tf), lambda i, j: (0, j)),
            pl.BlockSpec((tf, D_MODEL), lambda i, j: (j, 0)),
            pl.BlockSpec((tm, PLE_DIM), lambda i, j: (i, 0)),
            const((1, D_MODEL)),
            const((PLE_DIM, D_MODEL)),
            const((D_MODEL, D_MODEL)),
            const((1, D_MODEL)),
        ],
        out_specs=pl.BlockSpec((tm, D_MODEL), lambda i, j: (i, 0)),
        out_shape=jax.ShapeDtypeStruct((n, D_MODEL), F32),
        scratch_shapes=[pltpu.VMEM((tm, D_MODEL), BF16), pltpu.VMEM((tm, D_MODEL), F32)],
        compiler_params=_params("parallel", "arbitrary"),
        name="ffn_ple",
    )(h, g, wg, wu, wd, p, g_ple, w_ple, w_ple_gate, g_final)


def _moe_kernel(h_ref, g_ref, wr_ref, wg_ref, wu_ref, wd_ref, p_ref, gple_ref, wple_ref, wpg_ref,
                gfin_ref, out_ref, u_ref, acc_ref, gates_ref, *, final):
    e = pl.program_id(1)

    @pl.when(e == 0)
    def _():
        u = _rms(h_ref[...], g_ref[...])
        u_ref[...] = u.astype(BF16)
        acc_ref[...] = jnp.zeros_like(acc_ref)
        logits = jnp.dot(u, wr_ref[...], preferred_element_type=F32, precision=lax.Precision.HIGHEST)
        lane = lax.broadcasted_iota(jnp.int32, logits.shape, 1)
        logits = jnp.where(lane < N_EXPERTS, logits, -jnp.inf)
        m1 = jnp.max(logits, axis=-1, keepdims=True)
        i1 = jnp.min(jnp.where(logits == m1, lane, LANES), axis=-1, keepdims=True)
        rest = jnp.where(lane == i1, -jnp.inf, logits)
        m2 = jnp.max(rest, axis=-1, keepdims=True)
        i2 = jnp.min(jnp.where(rest == m2, lane, LANES), axis=-1, keepdims=True)
        e2 = jnp.exp(m2 - m1)
        w1 = 1.0 / (1.0 + e2)
        gates_ref[...] = jnp.where(lane == i1, w1, 0.0) + jnp.where(lane == i2, e2 * w1, 0.0)

    u = u_ref[...]
    act = _silu(_dot(u, wg_ref[0])) * _dot(u, wu_ref[0])
    lane = lax.broadcasted_iota(jnp.int32, gates_ref.shape, 1)
    gate = jnp.sum(jnp.where(lane == e, gates_ref[...], 0.0), axis=-1, keepdims=True)
    acc_ref[...] += gate * _dot(act.astype(BF16), wd_ref[0])

    @pl.when(e == pl.num_programs(1) - 1)
    def _():
        _ple_tail(h_ref[...] + acc_ref[...], p_ref, gple_ref, wple_ref, wpg_ref, gfin_ref, out_ref, final)


def _moe(h, g, w_router, wg, wu, wd, p, g_ple, w_ple, w_ple_gate, g_final, final, tm=512):
    n = h.shape[0]
    const = lambda shape: pl.BlockSpec(shape, lambda i, e: (0, 0))
    return pl.pallas_call(
        functools.partial(_moe_kernel, final=final),
        grid=(n // tm, N_EXPERTS),
        in_specs=[
            pl.BlockSpec((tm, D_MODEL), lambda i, e: (i, 0)),
            const((1, D_MODEL)),
            const((D_MODEL, LANES)),
            pl.BlockSpec((1, D_MODEL, D_FF_EXPERT), lambda i, e: (e, 0, 0)),
            pl.BlockSpec((1, D_MODEL, D_FF_EXPERT), lambda i, e: (e, 0, 0)),
            pl.BlockSpec((1, D_FF_EXPERT, D_MODEL), lambda i, e: (e, 0, 0)),
            pl.BlockSpec((tm, PLE_DIM), lambda i, e: (i, 0)),
            const((1, D_MODEL)),
            const((PLE_DIM, D_MODEL)),
            const((D_MODEL, D_MODEL)),
            const((1, D_MODEL)),
        ],
        out_specs=pl.BlockSpec((tm, D_MODEL), lambda i, e: (i, 0)),
        out_shape=jax.ShapeDtypeStruct((n, D_MODEL), F32),
        scratch_shapes=[
            pltpu.VMEM((tm, D_MODEL), BF16),
            pltpu.VMEM((tm, D_MODEL), F32),
            pltpu.VMEM((tm, LANES), F32),
        ],
        compiler_params=_params("parallel", "arbitrary"),
        name="moe_ple",
    )(h, g, w_router, wg, wu, wd, p, g_ple, w_ple, w_ple_gate, g_final)


def _rotary_tables(seq):
    half = RET_DK // 2
    pos = jnp.arange(seq, dtype=F32)
    inv_freq = ROPE_THETA ** (-jnp.arange(half, dtype=F32) / half)
    ang = pos[:, None] * inv_freq[None, :]
    cos, sin = jnp.cos(ang), jnp.sin(ang)
    return jnp.concatenate([cos, cos], axis=-1), jnp.concatenate([-sin, sin], axis=-1)


def kernel(x, p, w_in, conv_w, a_log, dt_bias, ret_gn, gdn_gn, w_br_ret, w_br_gdn, w_out, norm_mix,
           norm_ffn, norm_ple, w_ple, w_ple_gate, ffn_w_gate, ffn_w_up, ffn_w_down, router,
           exp_w_gate, exp_w_up, exp_w_down, norm_final):
    batch, seq, _ = x.shape
    depth = w_in.shape[0]
    n = batch * seq
    cos_t, sin_t = _rotary_tables(seq)
    g_final = norm_final.reshape(1, -1)
    h = x.reshape(n, D_MODEL)
    for i in range(depth):
        w_main = jnp.concatenate([w_in[i, :, :OFF_GA], w_in[i, :, OFF_MG:]], axis=1).astype(BF16)
        w_gab = jnp.pad(w_in[i, :, OFF_GA:OFF_MG], ((0, 0), (0, LANES - 2 * GDN_HEADS))).astype(BF16)
        proj, gab = _in_proj(h, norm_mix[i].reshape(1, -1), w_main, w_gab)
        o_ret = _retention(proj, cos_t, sin_t, ret_gn[i].reshape(1, -1), batch, seq)
        o_gdn = _gdn(proj, gab, conv_w[i], a_log[i], dt_bias[i], gdn_gn[i], batch, seq)
        h = _mix_out(h, o_ret, o_gdn, proj, w_br_ret[i].astype(BF16), w_br_gdn[i].astype(BF16),
                     w_out[i].astype(BF16))
        final = i == depth - 1
        tail = (p[i].reshape(n, PLE_DIM), norm_ple[i].reshape(1, -1), w_ple[i].astype(BF16),
                w_ple_gate[i].astype(BF16), g_final, final)
        j = i // 2
        g_ffn = norm_ffn[i].reshape(1, -1)
        if i % 2 == 0:
            h = _ffn(h, g_ffn, ffn_w_gate[j].astype(BF16), ffn_w_up[j].astype(BF16),
                     ffn_w_down[j].astype(BF16), *tail)
        else:
            w_router = jnp.pad(router[j], ((0, 0), (0, LANES - N_EXPERTS)))
            h = _moe(h, g_ffn, w_router, exp_w_gate[j].astype(BF16), exp_w_up[j].astype(BF16),
                     exp_w_down[j].astype(BF16), *tail)
    return h.reshape(batch, seq, D_MODEL)
```

```python
import functools

import numpy as np
import jax
import jax.numpy as jnp
from jax import lax
from jax.experimental import pallas as pl
from jax.experimental.pallas import tpu as pltpu

F32 = jnp.float32
BF16 = jnp.bfloat16

D_MODEL = 1024
PLE_DIM = 256
EPS = 1e-6
ROPE_THETA = 10000.0
RET_HEADS = 4
RET_DK = 128
RET_DV = 256
RET_QK = RET_HEADS * RET_DK
RET_V = RET_HEADS * RET_DV
GDN_HEADS = 8
GDN_DK = 128
GDN_DV = 128
GDN_QK = GDN_HEADS * GDN_DK
GDN_V = GDN_HEADS * GDN_DV
GDN_CONV_DIM = 2 * GDN_QK + GDN_V
CONV_WIDTH = 4
D_FF = 2816
N_EXPERTS = 8
D_FF_EXPERT = 1408
OFF_GA = 2 * RET_QK + 2 * RET_V + GDN_CONV_DIM + GDN_V
OFF_MG = OFF_GA + 2 * GDN_HEADS
N_MAIN = OFF_GA + 2 * D_MODEL
LANES = 128
GDN_CHUNK = 64
RET_CHUNK = 256
HALO = 8
VMEM_LIMIT = 56 * 1024 * 1024

_BLK_RV, _BLK_RG, _BLK_GQ, _BLK_GK, _BLK_GV, _BLK_GZ, _BLK_MGR, _BLK_MGG = 1, 2, 3, 4, 5, 6, 7, 8


def _silu(x):
    return x * jax.nn.sigmoid(x)


def _rms(x, g):
    return x * lax.rsqrt(jnp.mean(x * x, axis=-1, keepdims=True) + EPS) * g


def _dot(a, b):
    return jnp.dot(a, b, preferred_element_type=F32)


def _dot_nt(a, b):
    return lax.dot_general(a, b, (((1,), (1,)), ((), ())), preferred_element_type=F32)


def _params(*sem):
    return pltpu.CompilerParams(dimension_semantics=sem, vmem_limit_bytes=VMEM_LIMIT)


def _in_proj_kernel(x_ref, g_ref, wm_ref, wg_ref, om_ref, og_ref, u_ref, *, n_main):
    j = pl.program_id(1)

    @pl.when(j == 0)
    def _():
        u_ref[...] = _rms(x_ref[...], g_ref[...]).astype(BF16)

    @pl.when(j < n_main)
    def _():
        om_ref[...] = _dot(u_ref[...], wm_ref[...]).astype(BF16)

    @pl.when(j == n_main)
    def _():
        og_ref[...] = _dot(u_ref[...], wg_ref[...])


def _in_proj(h, g, w_main, w_gab, tm=1024, tn=1024):
    n = h.shape[0]
    n_main = N_MAIN // tn
    last = n_main - 1
    return pl.pallas_call(
        functools.partial(_in_proj_kernel, n_main=n_main),
        grid=(n // tm, n_main + 1),
        in_specs=[
            pl.BlockSpec((tm, D_MODEL), lambda i, j: (i, 0)),
            pl.BlockSpec((1, D_MODEL), lambda i, j: (0, 0)),
            pl.BlockSpec((D_MODEL, tn), lambda i, j: (0, jnp.minimum(j, last))),
            pl.BlockSpec((D_MODEL, LANES), lambda i, j: (0, 0)),
        ],
        out_specs=[
            pl.BlockSpec((tm, tn), lambda i, j: (i, jnp.minimum(j, last))),
            pl.BlockSpec((tm, LANES), lambda i, j: (i, 0)),
        ],
        out_shape=[
            jax.ShapeDtypeStruct((n, N_MAIN), BF16),
            jax.ShapeDtypeStruct((n, LANES), F32),
        ],
        scratch_shapes=[pltpu.VMEM((tm, D_MODEL), BF16)],
        compiler_params=_params("parallel", "arbitrary"),
        name="in_proj",
    )(h, g, w_main, w_gab)


def _ret_kernel(q_ref, k_ref, v_ref, rg_ref, cos_ref, sin_ref, dmask_ref, qdec_ref, kdec_ref,
                gn_ref, o_ref, state_ref, *, chunk_decay):
    @pl.when(pl.program_id(1) == 0)
    def _():
        state_ref[...] = jnp.zeros_like(state_ref)

    cos = cos_ref[...]
    sin = sin_ref[...]
    for hd in range(RET_HEADS):
        qk_sl = slice(hd * RET_DK, (hd + 1) * RET_DK)
        v_sl = slice(hd * RET_DV, (hd + 1) * RET_DV)
        q = q_ref[:, qk_sl].astype(F32)
        k = k_ref[:, qk_sl].astype(F32)
        q = q * cos + pltpu.roll(q, RET_DK // 2, 1) * sin
        k = (k * cos + pltpu.roll(k, RET_DK // 2, 1) * sin) * (RET_DK ** -0.5)
        v = v_ref[:, v_sl]
        scores = _dot_nt(q.astype(BF16), k.astype(BF16)) * dmask_ref[hd]
        state = state_ref[hd]
        o = _dot(scores.astype(BF16), v) + _dot((q * qdec_ref[:, qk_sl]).astype(BF16), state.astype(BF16))
        k_dec_t = (k * kdec_ref[:, qk_sl]).T.astype(BF16)
        state_ref[hd] = state * chunk_decay[hd] + _dot(k_dec_t, v)
        mu = jnp.mean(o, axis=-1, keepdims=True)
        oc = o - mu
        var = jnp.mean(oc * oc, axis=-1, keepdims=True)
        o = oc * lax.rsqrt(var + EPS) * gn_ref[:, v_sl]
        o_ref[:, v_sl] = (o * _silu(rg_ref[:, v_sl].astype(F32))).astype(BF16)


def _retention(proj, cos_t, sin_t, ret_gn, batch, seq):
    c = RET_CHUNK
    nc = seq // c
    log_gamma = np.log1p(-np.exp2(-5.0 - np.arange(RET_HEADS, dtype=np.float64)))
    chunk_decay = tuple(float(np.exp(c * lg)) for lg in log_gamma)
    lg = jnp.asarray(log_gamma, F32)
    idx = jnp.arange(c, dtype=F32)
    rel = idx[:, None] - idx[None, :]
    causal = rel >= 0
    dmask = jnp.where(causal[None], jnp.exp(jnp.where(causal, rel, 0.0)[None] * lg[:, None, None]), 0.0)
    qdec = jnp.repeat(jnp.exp((idx + 1.0)[:, None] * lg[None, :]), RET_DK, axis=1)
    kdec = jnp.repeat(jnp.exp((c - 1.0 - idx)[:, None] * lg[None, :]), RET_DK, axis=1)
    row = lambda b, t: b * nc + t
    return pl.pallas_call(
        functools.partial(_ret_kernel, chunk_decay=chunk_decay),
        grid=(batch, nc),
        in_specs=[
            pl.BlockSpec((c, RET_QK), lambda b, t: (row(b, t), 0)),
            pl.BlockSpec((c, RET_QK), lambda b, t: (row(b, t), 1)),
            pl.BlockSpec((c, RET_V), lambda b, t: (row(b, t), _BLK_RV)),
            pl.BlockSpec((c, RET_V), lambda b, t: (row(b, t), _BLK_RG)),
            pl.BlockSpec((c, RET_DK), lambda b, t: (t, 0)),
            pl.BlockSpec((c, RET_DK), lambda b, t: (t, 0)),
            pl.BlockSpec((RET_HEADS, c, c), lambda b, t: (0, 0, 0)),
            pl.BlockSpec((c, RET_QK), lambda b, t: (0, 0)),
            pl.BlockSpec((c, RET_QK), lambda b, t: (0, 0)),
            pl.BlockSpec((1, RET_V), lambda b, t: (0, 0)),
        ],
        out_specs=pl.BlockSpec((c, RET_V), lambda b, t: (row(b, t), 0)),
        out_shape=jax.ShapeDtypeStruct((batch * seq, RET_V), BF16),
        scratch_shapes=[pltpu.VMEM((RET_HEADS, RET_DK, RET_DV), F32)],
        compiler_params=_params("parallel", "arbitrary"),
        name="retention",
    )(proj, proj, proj, proj, cos_t, sin_t, dmask, qdec, kdec, ret_gn)


def _gdn_kernel(gq_ref, gk_ref, gv_ref, gz_ref, gab_ref, cw_ref, alog_ref, dtb_ref, gn_ref,
                o_ref, state_ref, buf_ref):
    c = GDN_CHUNK

    @pl.when(pl.program_id(1) == 0)
    def _():
        state_ref[...] = jnp.zeros_like(state_ref)
        buf_ref[0:HALO, :] = jnp.zeros((HALO, GDN_CONV_DIM), F32)

    buf_ref[HALO:HALO + c, 0:GDN_QK] = gq_ref[...].astype(F32)
    buf_ref[HALO:HALO + c, GDN_QK:2 * GDN_QK] = gk_ref[...].astype(F32)
    buf_ref[HALO:HALO + c, 2 * GDN_QK:GDN_CONV_DIM] = gv_ref[...].astype(F32)
    y = buf_ref[HALO:HALO + c, :] * cw_ref[CONV_WIDTH - 1:CONV_WIDTH, :]
    for s in range(1, CONV_WIDTH):
        y = y + buf_ref[HALO - s:HALO - s + c, :] * cw_ref[CONV_WIDTH - 1 - s:CONV_WIDTH - s, :]
    buf_ref[0:HALO, :] = buf_ref[c:c + HALO, :]
    y = _silu(y)

    gab = gab_ref[...]
    z = gab + dtb_ref[...]
    softplus = jnp.maximum(z, 0.0) + jnp.log(1.0 + jnp.exp(-jnp.abs(z)))
    g = -jnp.exp(alog_ref[...]) * softplus
    beta = jax.nn.sigmoid(gab)
    row = lax.broadcasted_iota(jnp.int32, (c, LANES), 0)
    gc = g
    for s in (1, 2, 4, 8, 16, 32):
        gc = gc + jnp.where(row >= s, pltpu.roll(gc, s, 0), 0.0)
    gc_t = gc.T

    ii = lax.broadcasted_iota(jnp.int32, (c, c), 0)
    jj = lax.broadcasted_iota(jnp.int32, (c, c), 1)
    incl = ii >= jj
    strict = ii > jj
    gn = gn_ref[...]
    hs = range(GDN_HEADS)
    sls = [slice(hd * GDN_DK, (hd + 1) * GDN_DK) for hd in hs]
    q_l, k_l, v_l, gcol_l, glast_l, bcol_l = [], [], [], [], [], []
    for hd in hs:
        q = y[:, sls[hd]]
        k = y[:, GDN_QK + hd * GDN_DK:GDN_QK + (hd + 1) * GDN_DK]
        v_l.append(y[:, 2 * GDN_QK + hd * GDN_DV:2 * GDN_QK + (hd + 1) * GDN_DV])
        q_l.append(q * lax.rsqrt(jnp.sum(q * q, axis=-1, keepdims=True) + EPS) * (GDN_DK ** -0.5))
        k_l.append(k * lax.rsqrt(jnp.sum(k * k, axis=-1, keepdims=True) + EPS))
        gcol_l.append(gc[:, hd:hd + 1])
        glast_l.append(gc[c - 1:c, hd:hd + 1])
        bcol_l.append(beta[:, GDN_HEADS + hd:GDN_HEADS + hd + 1])
    m_l, x_l, qk_l, ecol_l = [], [], [], []
    for hd in hs:
        decay = jnp.where(incl, jnp.exp(jnp.where(incl, gcol_l[hd] - gc_t[hd:hd + 1, :], 0.0)), 0.0)
        kb = k_l[hd] * bcol_l[hd]
        both = _dot_nt(jnp.concatenate([kb, q_l[hd]], axis=0).astype(BF16), k_l[hd].astype(BF16))
        m_l.append(-(both[0:c] * jnp.where(strict, decay, 0.0)))
        qk_l.append(both[c:2 * c] * decay)
        e_col = jnp.exp(gcol_l[hd])
        ecol_l.append(e_col)
        x_l.append(jnp.concatenate([v_l[hd] * bcol_l[hd], kb * e_col], axis=1))
    for it in range(6):
        mb_l = [m.astype(BF16) for m in m_l]
        x_l = [x_l[hd] + _dot(mb_l[hd], x_l[hd].astype(BF16)) for hd in hs]
        if it < 5:
            m_l = [_dot(mb_l[hd], mb_l[hd]) for hd in hs]
    st_l = [state_ref[hd] for hd in hs]
    ws_l = [_dot(jnp.concatenate([x_l[hd][:, GDN_DV:], q_l[hd] * ecol_l[hd]], axis=0).astype(BF16),
                 st_l[hd].astype(BF16)) for hd in hs]
    vn_l = [(x_l[hd][:, 0:GDN_DV] - ws_l[hd][0:c]).astype(BF16) for hd in hs]
    o_l = [ws_l[hd][c:2 * c] + _dot(qk_l[hd].astype(BF16), vn_l[hd]) for hd in hs]
    for hd in hs:
        k_dec_t = (k_l[hd] * jnp.exp(glast_l[hd] - gcol_l[hd])).T.astype(BF16)
        state_ref[hd] = st_l[hd] * jnp.exp(glast_l[hd]) + _dot(k_dec_t, vn_l[hd])
    for hd in hs:
        o = o_l[hd]
        o = o * lax.rsqrt(jnp.mean(o * o, axis=-1, keepdims=True) + EPS) * gn
        o_ref[:, sls[hd]] = (o * _silu(gz_ref[:, sls[hd]].astype(F32))).astype(BF16)


def _gdn(proj, gab, conv_w, a_log, dt_bias, gdn_gn, batch, seq):
    c = GDN_CHUNK
    nc = seq // c
    row = lambda b, t: b * nc + t
    pad = lambda a: jnp.pad(a.reshape(1, -1), ((0, 0), (0, LANES - a.shape[-1])))
    return pl.pallas_call(
        _gdn_kernel,
        grid=(batch, nc),
        in_specs=[
            pl.BlockSpec((c, GDN_QK), lambda b, t: (row(b, t), _BLK_GQ)),
            pl.BlockSpec((c, GDN_QK), lambda b, t: (row(b, t), _BLK_GK)),
            pl.BlockSpec((c, GDN_V), lambda b, t: (row(b, t), _BLK_GV)),
            pl.BlockSpec((c, GDN_V), lambda b, t: (row(b, t), _BLK_GZ)),
            pl.BlockSpec((c, LANES), lambda b, t: (row(b, t), 0)),
            pl.BlockSpec((CONV_WIDTH, GDN_CONV_DIM), lambda b, t: (0, 0)),
            pl.BlockSpec((1, LANES), lambda b, t: (0, 0)),
            pl.BlockSpec((1, LANES), lambda b, t: (0, 0)),
            pl.BlockSpec((1, GDN_DV), lambda b, t: (0, 0)),
        ],
        out_specs=pl.BlockSpec((c, GDN_V), lambda b, t: (row(b, t), 0)),
        out_shape=jax.ShapeDtypeStruct((batch * seq, GDN_V), BF16),
        scratch_shapes=[
            pltpu.VMEM((GDN_HEADS, GDN_DK, GDN_DV), F32),
            pltpu.VMEM((HALO + c, GDN_CONV_DIM), F32),
        ],
        compiler_params=_params("parallel", "arbitrary"),
        name="gdn",
    )(proj, proj, proj, proj, gab, conv_w, pad(a_log), pad(dt_bias), gdn_gn.reshape(1, -1))


def _mix_out_kernel(h_ref, oret_ref, ogdn_ref, mgr_ref, mgg_ref, wbr_ref, wbg_ref, wo_ref, out_ref):
    y_ret = _dot(oret_ref[...], wbr_ref[...])
    y_gdn = _dot(ogdn_ref[...], wbg_ref[...])
    merged = (jax.nn.sigmoid(mgr_ref[...].astype(F32)) * y_ret
              + jax.nn.sigmoid(mgg_ref[...].astype(F32)) * y_gdn)
    out_ref[...] = h_ref[...] + _dot(merged.astype(BF16), wo_ref[...])


def _mix_out(h, o_ret, o_gdn, proj, w_br_ret, w_br_gdn, w_out, tm=512):
    n = h.shape[0]
    tile = lambda blk: pl.BlockSpec((tm, D_MODEL), lambda i: (i, blk))
    weight = pl.BlockSpec((D_MODEL, D_MODEL), lambda i: (0, 0))
    return pl.pallas_call(
        _mix_out_kernel,
        grid=(n // tm,),
        in_specs=[tile(0), tile(0), tile(0), tile(_BLK_MGR), tile(_BLK_MGG), weight, weight, weight],
        out_specs=tile(0),
        out_shape=jax.ShapeDtypeStruct((n, D_MODEL), F32),
        compiler_params=_params("parallel"),
        name="mix_out",
    )(h, o_ret, o_gdn, proj, proj, w_br_ret, w_br_gdn, w_out)


def _ple_tail(h1, p_ref, gple_ref, wple_ref, wpg_ref, gfin_ref, out_ref, final):
    ple = _dot(p_ref[...].astype(BF16), wple_ref[...])
    gate = jax.nn.sigmoid(_dot(_rms(h1, gple_ref[...]).astype(BF16), wpg_ref[...]))
    h2 = h1 + gate * ple
    out_ref[...] = _rms(h2, gfin_ref[...]) if final else h2


def _ffn_kernel(h_ref, g_ref, wg_ref, wu_ref, wd_ref, p_ref, gple_ref, wple_ref, wpg_ref, gfin_ref,
                out_ref, u_ref, acc_ref, *, final):
    j = pl.program_id(1)

    @pl.when(j == 0)
    def _():
        u_ref[...] = _rms(h_ref[...], g_ref[...]).astype(BF16)
        acc_ref[...] = jnp.zeros_like(acc_ref)

    u = u_ref[...]
    act = _silu(_dot(u, wg_ref[...])) * _dot(u, wu_ref[...])
    acc_ref[...] += _dot(act.astype(BF16), wd_ref[...])

    @pl.when(j == pl.num_programs(1) - 1)
    def _():
        _ple_tail(h_ref[...] + acc_ref[...], p_ref, gple_ref, wple_ref, wpg_ref, gfin_ref, out_ref, final)


def _ffn(h, g, wg, wu, wd, p, g_ple, w_ple, w_ple_gate, g_final, final, tm=1024, tf=256):
    n = h.shape[0]
    const = lambda shape: pl.BlockSpec(shape, lambda i, j: (0, 0))
    return pl.pallas_call(
        functools.partial(_ffn_kernel, final=final),
        grid=(n // tm, D_FF // tf),
        in_specs=[
            pl.BlockSpec((tm, D_MODEL), lambda i, j: (i, 0)),
            const((1, D_MODEL)),
            pl.BlockSpec((D_MODEL, tf), lambda i, j: (0, j)),
            pl.BlockSpec((D_MODEL, tf), lambda i, j: (0, j)),
            pl.BlockSpec((tf, D_MODEL), lambda i, j: (j, 0)),
            pl.BlockSpec((tm, PLE_DIM), lambda i, j: (i, 0)),
            const((1, D_MODEL)),
            const((PLE_DIM, D_MODEL)),
            const((D_MODEL, D_MODEL)),
            const((1, D_MODEL)),
        ],
        out_specs=pl.BlockSpec((tm, D_MODEL), lambda i, j: (i, 0)),
        out_shape=jax.ShapeDtypeStruct((n, D_MODEL), F32),
        scratch_shapes=[pltpu.VMEM((tm, D_MODEL), BF16), pltpu.VMEM((tm, D_MODEL), F32)],
        compiler_params=_params("parallel", "arbitrary"),
        name="ffn_ple",
    )(h, g, wg, wu, wd, p, g_ple, w_ple, w_ple_gate, g_final)


def _moe_kernel(h_ref, g_ref, wr_ref, wg_ref, wu_ref, wd_ref, p_ref, gple_ref, wple_ref, wpg_ref,
                gfin_ref, out_ref, u_ref, acc_ref, gates_ref, *, final):
    e = pl.program_id(1)

    @pl.when(e == 0)
    def _():
        u = _rms(h_ref[...], g_ref[...])
        u_ref[...] = u.astype(BF16)
        acc_ref[...] = jnp.zeros_like(acc_ref)
        logits = jnp.dot(u, wr_ref[...], preferred_element_type=F32, precision=lax.Precision.HIGHEST)
        lane = lax.broadcasted_iota(jnp.int32, logits.shape, 1)
        logits = jnp.where(lane < N_EXPERTS, logits, -jnp.inf)
        m1 = jnp.max(logits, axis=-1, keepdims=True)
        i1 = jnp.min(jnp.where(logits == m1, lane, LANES), axis=-1, keepdims=True)
        rest = jnp.where(lane == i1, -jnp.inf, logits)
        m2 = jnp.max(rest, axis=-1, keepdims=True)
        i2 = jnp.min(jnp.where(rest == m2, lane, LANES), axis=-1, keepdims=True)
        e2 = jnp.exp(m2 - m1)
        w1 = 1.0 / (1.0 + e2)
        gates_ref[...] = jnp.where(lane == i1, w1, 0.0) + jnp.where(lane == i2, e2 * w1, 0.0)

    u = u_ref[...]
    act = _silu(_dot(u, wg_ref[0])) * _dot(u, wu_ref[0])
    lane = lax.broadcasted_iota(jnp.int32, gates_ref.shape, 1)
    gate = jnp.sum(jnp.where(lane == e, gates_ref[...], 0.0), axis=-1, keepdims=True)
    acc_ref[...] += gate * _dot(act.astype(BF16), wd_ref[0])

    @pl.when(e == pl.num_programs(1) - 1)
    def _():
        _ple_tail(h_ref[...] + acc_ref[...], p_ref, gple_ref, wple_ref, wpg_ref, gfin_ref, out_ref, final)


def _moe(h, g, w_router, wg, wu, wd, p, g_ple, w_ple, w_ple_gate, g_final, final, tm=512):
    n = h.shape[0]
    const = lambda shape: pl.BlockSpec(shape, lambda i, e: (0, 0))
    return pl.pallas_call(
        functools.partial(_moe_kernel, final=final),
        grid=(n // tm, N_EXPERTS),
        in_specs=[
            pl.BlockSpec((tm, D_MODEL), lambda i, e: (i, 0)),
            const((1, D_MODEL)),
            const((D_MODEL, LANES)),
            pl.BlockSpec((1, D_MODEL, D_FF_EXPERT), lambda i, e: (e, 0, 0)),
            pl.BlockSpec((1, D_MODEL, D_FF_EXPERT), lambda i, e: (e, 0, 0)),
            pl.BlockSpec((1, D_FF_EXPERT, D_MODEL), lambda i, e: (e, 0, 0)),
            pl.BlockSpec((tm, PLE_DIM), lambda i, e: (i, 0)),
            const((1, D_MODEL)),
            const((PLE_DIM, D_MODEL)),
            const((D_MODEL, D_MODEL)),
            const((1, D_MODEL)),
        ],
        out_specs=pl.BlockSpec((tm, D_MODEL), lambda i, e: (i, 0)),
        out_shape=jax.ShapeDtypeStruct((n, D_MODEL), F32),
        scratch_shapes=[
            pltpu.VMEM((tm, D_MODEL), BF16),
            pltpu.VMEM((tm, D_MODEL), F32),
            pltpu.VMEM((tm, LANES), F32),
        ],
        compiler_params=_params("parallel", "arbitrary"),
        name="moe_ple",
    )(h, g, w_router, wg, wu, wd, p, g_ple, w_ple, w_ple_gate, g_final)


def _rotary_tables(seq):
    half = RET_DK // 2
    pos = jnp.arange(seq, dtype=F32)
    inv_freq = ROPE_THETA ** (-jnp.arange(half, dtype=F32) / half)
    ang = pos[:, None] * inv_freq[None, :]
    cos, sin = jnp.cos(ang), jnp.sin(ang)
    return jnp.concatenate([cos, cos], axis=-1), jnp.concatenate([-sin, sin], axis=-1)


def kernel(x, p, w_in, conv_w, a_log, dt_bias, ret_gn, gdn_gn, w_br_ret, w_br_gdn, w_out, norm_mix,
           norm_ffn, norm_ple, w_ple, w_ple_gate, ffn_w_gate, ffn_w_up, ffn_w_down, router,
           exp_w_gate, exp_w_up, exp_w_down, norm_final):
    batch, seq, _ = x.shape
    depth = w_in.shape[0]
    n = batch * seq
    cos_t, sin_t = _rotary_tables(seq)
    g_final = norm_final.reshape(1, -1)
    h = x.reshape(n, D_MODEL)
    for i in range(depth):
        w_main = jnp.concatenate([w_in[i, :, :OFF_GA], w_in[i, :, OFF_MG:]], axis=1).astype(BF16)
        w_gab = jnp.pad(w_in[i, :, OFF_GA:OFF_MG], ((0, 0), (0, LANES - 2 * GDN_HEADS))).astype(BF16)
        proj, gab = _in_proj(h, norm_mix[i].reshape(1, -1), w_main, w_gab)
        o_ret = _retention(proj, cos_t, sin_t, ret_gn[i].reshape(1, -1), batch, seq)
        o_gdn = _gdn(proj, gab, conv_w[i], a_log[i], dt_bias[i], gdn_gn[i], batch, seq)
        h = _mix_out(h, o_ret, o_gdn, proj, w_br_ret[i].astype(BF16), w_br_gdn[i].astype(BF16),
                     w_out[i].astype(BF16))
        final = i == depth - 1
        tail = (p[i].reshape(n, PLE_DIM), norm_ple[i].reshape(1, -1), w_ple[i].astype(BF16),
                w_ple_gate[i].astype(BF16), g_final, final)
        j = i // 2
        g_ffn = norm_ffn[i].reshape(1, -1)
        if i % 2 == 0:
            h = _ffn(h, g_ffn, ffn_w_gate[j].astype(BF16), ffn_w_up[j].astype(BF16),
                     ffn_w_down[j].astype(BF16), *tail)
        else:
            w_router = jnp.pad(router[j], ((0, 0), (0, LANES - N_EXPERTS)))
            h = _moe(h, g_ffn, w_router, exp_w_gate[j].astype(BF16), exp_w_up[j].astype(BF16),
                     exp_w_down[j].astype(BF16), *tail)
    return h.reshape(batch, seq, D_MODEL)
```

```python
import functools

import numpy as np
import jax
import jax.numpy as jnp
from jax import lax
from jax.experimental import pallas as pl
from jax.experimental.pallas import tpu as pltpu

F32 = jnp.float32
BF16 = jnp.bfloat16

D_MODEL = 1024
PLE_DIM = 256
EPS = 1e-6
ROPE_THETA = 10000.0
RET_HEADS = 4
RET_DK = 128
RET_DV = 256
RET_QK = RET_HEADS * RET_DK
RET_V = RET_HEADS * RET_DV
GDN_HEADS = 8
GDN_DK = 128
GDN_DV = 128
GDN_QK = GDN_HEADS * GDN_DK
GDN_V = GDN_HEADS * GDN_DV
GDN_CONV_DIM = 2 * GDN_QK + GDN_V
CONV_WIDTH = 4
D_FF = 2816
N_EXPERTS = 8
D_FF_EXPERT = 1408
OFF_GA = 2 * RET_QK + 2 * RET_V + GDN_CONV_DIM + GDN_V
OFF_MG = OFF_GA + 2 * GDN_HEADS
N_MAIN = OFF_GA + 2 * D_MODEL
LANES = 128
GDN_CHUNK = 64
RET_CHUNK = 256
MOE_TILE = 512
HALO = 8
VMEM_LIMIT = 56 * 1024 * 1024

_BLK_RV, _BLK_RG, _BLK_GQ, _BLK_GK, _BLK_GV, _BLK_GZ, _BLK_MGR, _BLK_MGG = 1, 2, 3, 4, 5, 6, 7, 8


def _silu(x):
    return x * jax.nn.sigmoid(x)


def _rms(x, g):
    return x * lax.rsqrt(jnp.mean(x * x, axis=-1, keepdims=True) + EPS) * g


def _dot(a, b):
    return jnp.dot(a, b, preferred_element_type=F32)


def _dot_nt(a, b):
    return lax.dot_general(a, b, (((1,), (1,)), ((), ())), preferred_element_type=F32)


def _params(*sem):
    return pltpu.CompilerParams(dimension_semantics=sem, vmem_limit_bytes=VMEM_LIMIT)


def _in_proj_kernel(x_ref, g_ref, wm_ref, wg_ref, om_ref, og_ref, u_ref, *, n_main):
    j = pl.program_id(1)

    @pl.when(j == 0)
    def _():
        u_ref[...] = _rms(x_ref[...], g_ref[...]).astype(BF16)

    @pl.when(j < n_main)
    def _():
        om_ref[...] = _dot(u_ref[...], wm_ref[...]).astype(BF16)

    @pl.when(j == n_main)
    def _():
        og_ref[...] = _dot(u_ref[...], wg_ref[...])


def _in_proj(h, g, w_main, w_gab, tm=1024, tn=1024):
    n = h.shape[0]
    n_main = N_MAIN // tn
    last = n_main - 1
    return pl.pallas_call(
        functools.partial(_in_proj_kernel, n_main=n_main),
        grid=(n // tm, n_main + 1),
        in_specs=[
            pl.BlockSpec((tm, D_MODEL), lambda i, j: (i, 0)),
            pl.BlockSpec((1, D_MODEL), lambda i, j: (0, 0)),
            pl.BlockSpec((D_MODEL, tn), lambda i, j: (0, jnp.minimum(j, last))),
            pl.BlockSpec((D_MODEL, LANES), lambda i, j: (0, 0)),
        ],
        out_specs=[
            pl.BlockSpec((tm, tn), lambda i, j: (i, jnp.minimum(j, last))),
            pl.BlockSpec((tm, LANES), lambda i, j: (i, 0)),
        ],
        out_shape=[
            jax.ShapeDtypeStruct((n, N_MAIN), BF16),
            jax.ShapeDtypeStruct((n, LANES), F32),
        ],
        scratch_shapes=[pltpu.VMEM((tm, D_MODEL), BF16)],
        compiler_params=_params("parallel", "arbitrary"),
        name="in_proj",
    )(h, g, w_main, w_gab)


def _ret_kernel(q_ref, k_ref, v_ref, rg_ref, cos_ref, sin_ref, dmask_ref, qdec_ref, kdec_ref,
                gn_ref, o_ref, state_ref, *, chunk_decay):
    @pl.when(pl.program_id(1) == 0)
    def _():
        state_ref[...] = jnp.zeros_like(state_ref)

    cos = cos_ref[...]
    sin = sin_ref[...]
    for hd in range(RET_HEADS):
        qk_sl = slice(hd * RET_DK, (hd + 1) * RET_DK)
        v_sl = slice(hd * RET_DV, (hd + 1) * RET_DV)
        q = q_ref[:, qk_sl].astype(F32)
        k = k_ref[:, qk_sl].astype(F32)
        q = q * cos + pltpu.roll(q, RET_DK // 2, 1) * sin
        k = (k * cos + pltpu.roll(k, RET_DK // 2, 1) * sin) * (RET_DK ** -0.5)
        v = v_ref[:, v_sl]
        scores = _dot_nt(q.astype(BF16), k.astype(BF16)) * dmask_ref[hd]
        state = state_ref[hd]
        o = _dot(scores.astype(BF16), v) + _dot((q * qdec_ref[:, qk_sl]).astype(BF16), state.astype(BF16))
        k_dec_t = (k * kdec_ref[:, qk_sl]).T.astype(BF16)
        state_ref[hd] = state * chunk_decay[hd] + _dot(k_dec_t, v)
        mu = jnp.mean(o, axis=-1, keepdims=True)
        oc = o - mu
        var = jnp.mean(oc * oc, axis=-1, keepdims=True)
        o = oc * lax.rsqrt(var + EPS) * gn_ref[:, v_sl]
        o_ref[:, v_sl] = (o * _silu(rg_ref[:, v_sl].astype(F32))).astype(BF16)


def _retention(proj, cos_t, sin_t, ret_gn, batch, seq):
    c = RET_CHUNK
    nc = seq // c
    log_gamma = np.log1p(-np.exp2(-5.0 - np.arange(RET_HEADS, dtype=np.float64)))
    chunk_decay = tuple(float(np.exp(c * lg)) for lg in log_gamma)
    lg = jnp.asarray(log_gamma, F32)
    idx = jnp.arange(c, dtype=F32)
    rel = idx[:, None] - idx[None, :]
    causal = rel >= 0
    dmask = jnp.where(causal[None], jnp.exp(jnp.where(causal, rel, 0.0)[None] * lg[:, None, None]), 0.0)
    qdec = jnp.repeat(jnp.exp((idx + 1.0)[:, None] * lg[None, :]), RET_DK, axis=1)
    kdec = jnp.repeat(jnp.exp((c - 1.0 - idx)[:, None] * lg[None, :]), RET_DK, axis=1)
    row = lambda b, t: b * nc + t
    return pl.pallas_call(
        functools.partial(_ret_kernel, chunk_decay=chunk_decay),
        grid=(batch, nc),
        in_specs=[
            pl.BlockSpec((c, RET_QK), lambda b, t: (row(b, t), 0)),
            pl.BlockSpec((c, RET_QK), lambda b, t: (row(b, t), 1)),
            pl.BlockSpec((c, RET_V), lambda b, t: (row(b, t), _BLK_RV)),
            pl.BlockSpec((c, RET_V), lambda b, t: (row(b, t), _BLK_RG)),
            pl.BlockSpec((c, RET_DK), lambda b, t: (t, 0)),
            pl.BlockSpec((c, RET_DK), lambda b, t: (t, 0)),
            pl.BlockSpec((RET_HEADS, c, c), lambda b, t: (0, 0, 0)),
            pl.BlockSpec((c, RET_QK), lambda b, t: (0, 0)),
            pl.BlockSpec((c, RET_QK), lambda b, t: (0, 0)),
            pl.BlockSpec((1, RET_V), lambda b, t: (0, 0)),
        ],
        out_specs=pl.BlockSpec((c, RET_V), lambda b, t: (row(b, t), 0)),
        out_shape=jax.ShapeDtypeStruct((batch * seq, RET_V), BF16),
        scratch_shapes=[pltpu.VMEM((RET_HEADS, RET_DK, RET_DV), F32)],
        compiler_params=_params("parallel", "arbitrary"),
        name="retention",
    )(proj, proj, proj, proj, cos_t, sin_t, dmask, qdec, kdec, ret_gn)


def _gdn_prepare_pieces(slot, gq_ref, gk_ref, gv_ref, gab_ref, cw_ref, alog_ref, dtb_ref, buf_ref, qkv_ref,
                        gate_ref):
    c = GDN_CHUNK

    def column_block(j):
        src_ref = (gq_ref, gk_ref, gv_ref)[j // GDN_HEADS]
        src_sl = slice((j % GDN_HEADS) * LANES, (j % GDN_HEADS + 1) * LANES)
        sl = slice(j * LANES, (j + 1) * LANES)
        buf_ref[HALO:HALO + c, sl] = src_ref[:, src_sl].astype(F32)
        y = buf_ref[HALO:HALO + c, sl] * cw_ref[CONV_WIDTH - 1:CONV_WIDTH, sl]
        for s in range(1, CONV_WIDTH):
            y = y + buf_ref[HALO - s:HALO - s + c, sl] * cw_ref[CONV_WIDTH - 1 - s:CONV_WIDTH - s, sl]
        buf_ref[0:HALO, sl] = buf_ref[c:c + HALO, sl]
        y = _silu(y)
        if j < 2 * GDN_HEADS:
            scale = GDN_DK ** -0.5 if j < GDN_HEADS else 1.0
            y = y * (lax.rsqrt(jnp.sum(y * y, axis=-1, keepdims=True) + EPS) * scale)
        qkv_ref[slot, :, sl] = y

    def gates():
        gab = gab_ref[...]
        z = gab + dtb_ref[...]
        softplus = jnp.maximum(z, 0.0) + jnp.log(1.0 + jnp.exp(-jnp.abs(z)))
        gc = -jnp.exp(alog_ref[...]) * softplus
        row = lax.broadcasted_iota(jnp.int32, (c, LANES), 0)
        for s in (1, 2, 4, 8, 16, 32):
            gc = gc + jnp.where(row >= s, pltpu.roll(gc, s, 0), 0.0)
        gate_ref[slot, 0] = gc
        gate_ref[slot, 1] = jax.nn.sigmoid(gab)

    return [functools.partial(column_block, j) for j in range(GDN_CONV_DIM // LANES)] + [gates]


def _gdn_delta_rule_stages(slot, gz_ref, gn_ref, o_ref, state_ref, qkv_ref, gate_ref):
    c = GDN_CHUNK
    gc = gate_ref[slot, 0]
    beta = gate_ref[slot, 1]
    gc_t = gc.T
    ii = lax.broadcasted_iota(jnp.int32, (c, c), 0)
    jj = lax.broadcasted_iota(jnp.int32, (c, c), 1)
    incl = ii >= jj
    strict = ii > jj
    gn = gn_ref[...]
    hs = range(GDN_HEADS)
    sls = [slice(hd * GDN_DK, (hd + 1) * GDN_DK) for hd in hs]
    q_l = [qkv_ref[slot, :, sls[hd]] for hd in hs]
    k_l = [qkv_ref[slot, :, GDN_QK + hd * GDN_DK:GDN_QK + (hd + 1) * GDN_DK] for hd in hs]
    v_l = [qkv_ref[slot, :, 2 * GDN_QK + hd * GDN_DV:2 * GDN_QK + (hd + 1) * GDN_DV] for hd in hs]
    gcol_l = [gc[:, hd:hd + 1] for hd in hs]
    glast_l = [gc[c - 1:c, hd:hd + 1] for hd in hs]
    bcol_l = [beta[:, GDN_HEADS + hd:GDN_HEADS + hd + 1] for hd in hs]
    m_l, x_l, qk_l, ecol_l = [], [], [], []
    for hd in hs:
        decay = jnp.where(incl, jnp.exp(jnp.where(incl, gcol_l[hd] - gc_t[hd:hd + 1, :], 0.0)), 0.0)
        kb = k_l[hd] * bcol_l[hd]
        both = _dot_nt(jnp.concatenate([kb, q_l[hd]], axis=0).astype(BF16), k_l[hd].astype(BF16))
        m_l.append(-(both[0:c] * jnp.where(strict, decay, 0.0)))
        qk_l.append(both[c:2 * c] * decay)
        e_col = jnp.exp(gcol_l[hd])
        ecol_l.append(e_col)
        x_l.append(jnp.concatenate([v_l[hd] * bcol_l[hd], kb * e_col], axis=1))
    yield
    for it in range(6):
        mb_l = [m.astype(BF16) for m in m_l]
        x_l = [x_l[hd] + _dot(mb_l[hd], x_l[hd].astype(BF16)) for hd in hs]
        if it < 5:
            m_l = [_dot(mb_l[hd], mb_l[hd]) for hd in hs]
        yield
    st_l = [state_ref[hd] for hd in hs]
    ws_l = [_dot(jnp.concatenate([x_l[hd][:, GDN_DV:], q_l[hd] * ecol_l[hd]], axis=0).astype(BF16),
                 st_l[hd].astype(BF16)) for hd in hs]
    vn_l = [(x_l[hd][:, 0:GDN_DV] - ws_l[hd][0:c]).astype(BF16) for hd in hs]
    yield
    o_l = [ws_l[hd][c:2 * c] + _dot(qk_l[hd].astype(BF16), vn_l[hd]) for hd in hs]
    for hd in hs:
        k_dec_t = (k_l[hd] * jnp.exp(glast_l[hd] - gcol_l[hd])).T.astype(BF16)
        state_ref[hd] = st_l[hd] * jnp.exp(glast_l[hd]) + _dot(k_dec_t, vn_l[hd])
    yield
    for hd in hs:
        o = o_l[hd]
        o = o * lax.rsqrt(jnp.mean(o * o, axis=-1, keepdims=True) + EPS) * gn
        o_ref[:, sls[hd]] = (o * _silu(gz_ref[:, sls[hd]].astype(F32))).astype(BF16)


def _gdn_kernel(gq_ref, gk_ref, gv_ref, gz_ref, gab_ref, cw_ref, alog_ref, dtb_ref, gn_ref,
                o_ref, state_ref, buf_ref, qkv_ref, gate_ref):
    t = pl.program_id(1)

    @pl.when(t == 0)
    def _():
        state_ref[...] = jnp.zeros_like(state_ref)
        buf_ref[0:HALO, :] = jnp.zeros((HALO, GDN_CONV_DIM), F32)
        qkv_ref[...] = jnp.zeros_like(qkv_ref)
        gate_ref[...] = jnp.zeros_like(gate_ref)

    slot = t % 2
    pieces = _gdn_prepare_pieces(slot, gq_ref, gk_ref, gv_ref, gab_ref, cw_ref, alog_ref, dtb_ref, buf_ref,
                                 qkv_ref, gate_ref)
    per_stage = 3
    for _ in _gdn_delta_rule_stages(1 - slot, gz_ref, gn_ref, o_ref, state_ref, qkv_ref, gate_ref):
        for piece in pieces[:per_stage]:
            piece()
        pieces = pieces[per_stage:]
    for piece in pieces:
        piece()


def _gdn(proj, gab, conv_w, a_log, dt_bias, gdn_gn, batch, seq):
    c = GDN_CHUNK
    nc = seq // c
    row = lambda b, t: b * nc + jnp.minimum(t, nc - 1)
    prev = lambda b, t: b * nc + jnp.maximum(t - 1, 0)
    pad = lambda a: jnp.pad(a.reshape(1, -1), ((0, 0), (0, LANES - a.shape[-1])))
    return pl.pallas_call(
        _gdn_kernel,
        grid=(batch, nc + 1),
        in_specs=[
            pl.BlockSpec((c, GDN_QK), lambda b, t: (row(b, t), _BLK_GQ)),
            pl.BlockSpec((c, GDN_QK), lambda b, t: (row(b, t), _BLK_GK)),
            pl.BlockSpec((c, GDN_V), lambda b, t: (row(b, t), _BLK_GV)),
            pl.BlockSpec((c, GDN_V), lambda b, t: (prev(b, t), _BLK_GZ)),
            pl.BlockSpec((c, LANES), lambda b, t: (row(b, t), 0)),
            pl.BlockSpec((CONV_WIDTH, GDN_CONV_DIM), lambda b, t: (0, 0)),
            pl.BlockSpec((1, LANES), lambda b, t: (0, 0)),
            pl.BlockSpec((1, LANES), lambda b, t: (0, 0)),
            pl.BlockSpec((1, GDN_DV), lambda b, t: (0, 0)),
        ],
        out_specs=pl.BlockSpec((c, GDN_V), lambda b, t: (prev(b, t), 0)),
        out_shape=jax.ShapeDtypeStruct((batch * seq, GDN_V), BF16),
        scratch_shapes=[
            pltpu.VMEM((GDN_HEADS, GDN_DK, GDN_DV), F32),
            pltpu.VMEM((HALO + c, GDN_CONV_DIM), F32),
            pltpu.VMEM((2, c, GDN_CONV_DIM), F32),
            pltpu.VMEM((2, 2, c, LANES), F32),
        ],
        compiler_params=_params("parallel", "arbitrary"),
        name="gdn",
    )(proj, proj, proj, proj, gab, conv_w, pad(a_log), pad(dt_bias), gdn_gn.reshape(1, -1))


def _mix_out_kernel(h_ref, oret_ref, ogdn_ref, mgr_ref, mgg_ref, wbr_ref, wbg_ref, wo_ref, out_ref):
    y_ret = _dot(oret_ref[...], wbr_ref[...])
    y_gdn = _dot(ogdn_ref[...], wbg_ref[...])
    merged = (jax.nn.sigmoid(mgr_ref[...].astype(F32)) * y_ret
              + jax.nn.sigmoid(mgg_ref[...].astype(F32)) * y_gdn)
    out_ref[...] = h_ref[...] + _dot(merged.astype(BF16), wo_ref[...])


def _mix_out(h, o_ret, o_gdn, proj, w_br_ret, w_br_gdn, w_out, tm=512):
    n = h.shape[0]
    tile = lambda blk: pl.BlockSpec((tm, D_MODEL), lambda i: (i, blk))
    weight = pl.BlockSpec((D_MODEL, D_MODEL), lambda i: (0, 0))
    return pl.pallas_call(
        _mix_out_kernel,
        grid=(n // tm,),
        in_specs=[tile(0), tile(0), tile(0), tile(_BLK_MGR), tile(_BLK_MGG), weight, weight, weight],
        out_specs=tile(0),
        out_shape=jax.ShapeDtypeStruct((n, D_MODEL), F32),
        compiler_params=_params("parallel"),
        name="mix_out",
    )(h, o_ret, o_gdn, proj, proj, w_br_ret, w_br_gdn, w_out)


def _ple_tail(h1, p_ref, gple_ref, wple_ref, wpg_ref, gfin_ref, out_ref, final):
    ple = _dot(p_ref[...].astype(BF16), wple_ref[...])
    gate = jax.nn.sigmoid(_dot(_rms(h1, gple_ref[...]).astype(BF16), wpg_ref[...]))
    h2 = h1 + gate * ple
    out_ref[...] = _rms(h2, gfin_ref[...]) if final else h2


def _ffn_kernel(h_ref, g_ref, wg_ref, wu_ref, wd_ref, p_ref, gple_ref, wple_ref, wpg_ref, gfin_ref,
                out_ref, u_ref, acc_ref, *, final):
    j = pl.program_id(1)

    @pl.when(j == 0)
    def _():
        u_ref[...] = _rms(h_ref[...], g_ref[...]).astype(BF16)
        acc_ref[...] = jnp.zeros_like(acc_ref)

    u = u_ref[...]
    act = _silu(_dot(u, wg_ref[...])) * _dot(u, wu_ref[...])
    acc_ref[...] += _dot(act.astype(BF16), wd_ref[...])

    @pl.when(j == pl.num_programs(1) - 1)
    def _():
        _ple_tail(h_ref[...] + acc_ref[...], p_ref, gple_ref, wple_ref, wpg_ref, gfin_ref, out_ref, final)


def _ffn(h, g, wg, wu, wd, p, g_ple, w_ple, w_ple_gate, g_final, final, tm=1024, tf=256):
    n = h.shape[0]
    const = lambda shape: pl.BlockSpec(shape, lambda i, j: (0, 0))
    return pl.pallas_call(
        functools.partial(_ffn_kernel, final=final),
        grid=(n // tm, D_FF // tf),
        in_specs=[
            pl.BlockSpec((tm, D_MODEL), lambda i, j: (i, 0)),
            const((1, D_MODEL)),
            pl.BlockSpec((D_MODEL, tf), lambda i, j: (0, j)),
            pl.BlockSpec((D_MODEL, tf), lambda i, j: (0, j)),
            pl.BlockSpec((tf, D_MODEL), lambda i, j: (j, 0)),
            pl.BlockSpec((tm, PLE_DIM), lambda i, j: (i, 0)),
            const((1, D_MODEL)),
            const((PLE_DIM, D_MODEL)),
            const((D_MODEL, D_MODEL)),
            const((1, D_MODEL)),
        ],
        out_specs=pl.BlockSpec((tm, D_MODEL), lambda i, j: (i, 0)),
        out_shape=jax.ShapeDtypeStruct((n, D_MODEL), F32),
        scratch_shapes=[pltpu.VMEM((tm, D_MODEL), BF16), pltpu.VMEM((tm, D_MODEL), F32)],
        compiler_params=_params("parallel", "arbitrary"),
        name="ffn_ple",
    )(h, g, wg, wu, wd, p, g_ple, w_ple, w_ple_gate, g_final)


def _router_kernel(h_ref, g_ref, wr_ref, route_ref):
    u = _rms(h_ref[...], g_ref[...])
    logits = jnp.dot(u, wr_ref[...], preferred_element_type=F32, precision=lax.Precision.HIGHEST)
    lane = lax.broadcasted_iota(jnp.int32, logits.shape, 1)
    logits = jnp.where(lane < N_EXPERTS, logits, -jnp.inf)
    m1 = jnp.max(logits, axis=-1, keepdims=True)
    i1 = jnp.min(jnp.where(logits == m1, lane, LANES), axis=-1, keepdims=True)
    rest = jnp.where(lane == i1, -jnp.inf, logits)
    m2 = jnp.max(rest, axis=-1, keepdims=True)
    i2 = jnp.min(jnp.where(rest == m2, lane, LANES), axis=-1, keepdims=True)
    e2 = jnp.exp(m2 - m1)
    w1 = 1.0 / (1.0 + e2)
    route_ref[...] = (jnp.where(lane == 0, i1.astype(F32), 0.0) + jnp.where(lane == 1, i2.astype(F32), 0.0)
                      + jnp.where(lane == 2, w1, 0.0) + jnp.where(lane == 3, e2 * w1, 0.0))


def _router(h, g, w_router, tm=1024):
    n = h.shape[0]
    return pl.pallas_call(
        _router_kernel,
        grid=(n // tm,),
        in_specs=[
            pl.BlockSpec((tm, D_MODEL), lambda i: (i, 0)),
            pl.BlockSpec((1, D_MODEL), lambda i: (0, 0)),
            pl.BlockSpec((D_MODEL, LANES), lambda i: (0, 0)),
        ],
        out_specs=pl.BlockSpec((tm, LANES), lambda i: (i, 0)),
        out_shape=jax.ShapeDtypeStruct((n, LANES), F32),
        compiler_params=_params("parallel"),
        name="router",
    )(h, g, w_router)


def _routing_tables(route, n):
    expert = route[:, 0:2].astype(jnp.int32).reshape(-1)
    onehot = (expert[:, None] == jnp.arange(N_EXPERTS, dtype=jnp.int32)[None, :]).astype(jnp.int32)
    csum = jnp.cumsum(onehot, axis=0)
    rank = jnp.sum(csum * onehot, axis=1) - 1
    counts = csum[-1]
    padded = ((counts + MOE_TILE - 1) // MOE_TILE) * MOE_TILE
    group_end = jnp.cumsum(padded)
    pos = (group_end - padded)[expert] + rank
    n_rows = 2 * n + N_EXPERTS * MOE_TILE
    src_tok = jnp.zeros((n_rows,), jnp.int32).at[pos].set(jnp.arange(2 * n, dtype=jnp.int32) // 2)
    n_used = (group_end[-1] // MOE_TILE).reshape(1)
    tile_start = jnp.arange(n_rows // MOE_TILE, dtype=jnp.int32) * MOE_TILE
    tile_expert = jnp.minimum(jnp.searchsorted(group_end, tile_start, side="right"), N_EXPERTS - 1)
    return src_tok, pos.astype(jnp.int32), tile_expert.astype(jnp.int32), n_used.astype(jnp.int32)


def _row_gather_start(idx_ref, base, count, stride, src_hbm, dst_ref, sem):
    def body(r, carry):
        row = idx_ref[base + stride * r]
        pltpu.make_async_copy(src_hbm.at[pl.ds(row, 1)], dst_ref.at[pl.ds(r, 1)], sem).start()
        return carry
    lax.fori_loop(0, count, body, 0, unroll=8)


def _row_gather_wait(src_hbm, dst_ref, sem):
    pltpu.make_async_copy(src_hbm.at[pl.ds(0, dst_ref.shape[0])], dst_ref, sem).wait()


def _moe_gemm_kernel(src_tok_ref, tile_expert_ref, n_used_ref, h_hbm, g_ref, wg_ref, wu_ref, wd_ref,
                     ys_ref, xs_ref, sem):
    del tile_expert_ref
    i = pl.program_id(0)
    n_used = n_used_ref[0]

    @pl.when(i == 0)
    def _():
        _row_gather_start(src_tok_ref, 0, MOE_TILE, 1, h_hbm, xs_ref.at[0], sem.at[0])

    @pl.when(i + 1 < n_used)
    def _():
        nxt = (i + 1) % 2
        _row_gather_start(src_tok_ref, (i + 1) * MOE_TILE, MOE_TILE, 1, h_hbm, xs_ref.at[nxt], sem.at[nxt])

    @pl.when(i < n_used)
    def _():
        cur = i % 2
        _row_gather_wait(h_hbm, xs_ref.at[cur], sem.at[cur])
        u = _rms(xs_ref[cur], g_ref[...]).astype(BF16)
        act = _silu(_dot(u, wg_ref[0])) * _dot(u, wu_ref[0])
        ys_ref[...] = _dot(act.astype(BF16), wd_ref[0])

    @pl.when(i >= n_used)
    def _():
        ys_ref[...] = jnp.zeros_like(ys_ref)


def _moe_gemm(h, g, src_tok, tile_expert, n_used, wg, wu, wd):
    n_rows = src_tok.shape[0]
    n_tiles = n_rows // MOE_TILE
    expert_block = lambda i, st, te, nu: (te[jnp.minimum(i, nu[0] - 1)], 0, 0)
    return pl.pallas_call(
        _moe_gemm_kernel,
        grid_spec=pltpu.PrefetchScalarGridSpec(
            num_scalar_prefetch=3,
            grid=(n_tiles,),
            in_specs=[
                pl.BlockSpec(memory_space=pl.ANY),
                pl.BlockSpec((1, D_MODEL), lambda i, st, te, nu: (0, 0)),
                pl.BlockSpec((1, D_MODEL, D_FF_EXPERT), expert_block),
                pl.BlockSpec((1, D_MODEL, D_FF_EXPERT), expert_block),
                pl.BlockSpec((1, D_FF_EXPERT, D_MODEL), expert_block),
            ],
            out_specs=pl.BlockSpec((MOE_TILE, D_MODEL), lambda i, st, te, nu: (i, 0)),
            scratch_shapes=[pltpu.VMEM((2, MOE_TILE, D_MODEL), F32), pltpu.SemaphoreType.DMA((2,))],
        ),
        out_shape=jax.ShapeDtypeStruct((n_rows, D_MODEL), F32),
        compiler_params=_params("arbitrary"),
        name="moe_gemm",
    )(src_tok, tile_expert, n_used, h, g, wg, wu, wd)


def _moe_tail_kernel(pos_ref, h_ref, route_ref, ys_hbm, p_ref, gple_ref, wple_ref, wpg_ref, gfin_ref,
                     out_ref, y_ref, sem, *, final, tm):
    i = pl.program_id(0)

    def start(tile, slot):
        for s in range(2):
            _row_gather_start(pos_ref, 2 * tile * tm + s, tm, 2, ys_hbm, y_ref.at[slot, s], sem.at[slot, s])

    @pl.when(i == 0)
    def _():
        start(0, 0)

    @pl.when(i + 1 < pl.num_programs(0))
    def _():
        start(i + 1, (i + 1) % 2)

    cur = i % 2
    for s in range(2):
        _row_gather_wait(ys_hbm, y_ref.at[cur, s], sem.at[cur, s])
    route = route_ref[...]
    h1 = h_ref[...] + route[:, 2:3] * y_ref[cur, 0] + route[:, 3:4] * y_ref[cur, 1]
    _ple_tail(h1, p_ref, gple_ref, wple_ref, wpg_ref, gfin_ref, out_ref, final)


def _moe_tail(h, route, pos, ys, p, g_ple, w_ple, w_ple_gate, g_final, final, tm=512):
    n = h.shape[0]
    const = lambda shape: pl.BlockSpec(shape, lambda i, pos: (0, 0))
    return pl.pallas_call(
        functools.partial(_moe_tail_kernel, final=final, tm=tm),
        grid_spec=pltpu.PrefetchScalarGridSpec(
            num_scalar_prefetch=1,
            grid=(n // tm,),
            in_specs=[
                pl.BlockSpec((tm, D_MODEL), lambda i, pos: (i, 0)),
                pl.BlockSpec((tm, LANES), lambda i, pos: (i, 0)),
                pl.BlockSpec(memory_space=pl.ANY),
                pl.BlockSpec((tm, PLE_DIM), lambda i, pos: (i, 0)),
                const((1, D_MODEL)),
                const((PLE_DIM, D_MODEL)),
                const((D_MODEL, D_MODEL)),
                const((1, D_MODEL)),
            ],
            out_specs=pl.BlockSpec((tm, D_MODEL), lambda i, pos: (i, 0)),
            scratch_shapes=[pltpu.VMEM((2, 2, tm, D_MODEL), F32), pltpu.SemaphoreType.DMA((2, 2))],
        ),
        out_shape=jax.ShapeDtypeStruct((n, D_MODEL), F32),
        compiler_params=_params("arbitrary"),
        name="moe_tail",
    )(pos, h, route, ys, p, g_ple, w_ple, w_ple_gate, g_final)


def _moe(h, g, w_router, wg, wu, wd, p, g_ple, w_ple, w_ple_gate, g_final, final):
    n = h.shape[0]
    route = _router(h, g, w_router)
    src_tok, pos, tile_expert, n_used = _routing_tables(route, n)
    ys = _moe_gemm(h, g, src_tok, tile_expert, n_used, wg, wu, wd)
    return _moe_tail(h, route, pos, ys, p, g_ple, w_ple, w_ple_gate, g_final, final)


def _rotary_tables(seq):
    half = RET_DK // 2
    pos = jnp.arange(seq, dtype=F32)
    inv_freq = ROPE_THETA ** (-jnp.arange(half, dtype=F32) / half)
    ang = pos[:, None] * inv_freq[None, :]
    cos, sin = jnp.cos(ang), jnp.sin(ang)
    return jnp.concatenate([cos, cos], axis=-1), jnp.concatenate([-sin, sin], axis=-1)


def kernel(x, p, w_in, conv_w, a_log, dt_bias, ret_gn, gdn_gn, w_br_ret, w_br_gdn, w_out, norm_mix,
           norm_ffn, norm_ple, w_ple, w_ple_gate, ffn_w_gate, ffn_w_up, ffn_w_down, router,
           exp_w_gate, exp_w_up, exp_w_down, norm_final):
    batch, seq, _ = x.shape
    depth = w_in.shape[0]
    n = batch * seq
    cos_t, sin_t = _rotary_tables(seq)
    g_final = norm_final.reshape(1, -1)
    h = x.reshape(n, D_MODEL)
    for i in range(depth):
        w_main = jnp.concatenate([w_in[i, :, :OFF_GA], w_in[i, :, OFF_MG:]], axis=1).astype(BF16)
        w_gab = jnp.pad(w_in[i, :, OFF_GA:OFF_MG], ((0, 0), (0, LANES - 2 * GDN_HEADS))).astype(BF16)
        proj, gab = _in_proj(h, norm_mix[i].reshape(1, -1), w_main, w_gab)
        o_ret = _retention(proj, cos_t, sin_t, ret_gn[i].reshape(1, -1), batch, seq)
        o_gdn = _gdn(proj, gab, conv_w[i], a_log[i], dt_bias[i], gdn_gn[i], batch, seq)
        h = _mix_out(h, o_ret, o_gdn, proj, w_br_ret[i].astype(BF16), w_br_gdn[i].astype(BF16),
                     w_out[i].astype(BF16))
        final = i == depth - 1
        tail = (p[i].reshape(n, PLE_DIM), norm_ple[i].reshape(1, -1), w_ple[i].astype(BF16),
                w_ple_gate[i].astype(BF16), g_final, final)
        j = i // 2
        g_ffn = norm_ffn[i].reshape(1, -1)
        if i % 2 == 0:
            h = _ffn(h, g_ffn, ffn_w_gate[j].astype(BF16), ffn_w_up[j].astype(BF16),
                     ffn_w_down[j].astype(BF16), *tail)
        else:
            w_router = jnp.pad(router[j], ((0, 0), (0, LANES - N_EXPERTS)))
            h = _moe(h, g_ffn, w_router, exp_w_gate[j].astype(BF16), exp_w_up[j].astype(BF16),
                     exp_w_down[j].astype(BF16), *tail)
    return h.reshape(batch, seq, D_MODEL)
```

```python
import functools

import numpy as np
import jax
import jax.numpy as jnp
from jax import lax
from jax.experimental import pallas as pl
from jax.experimental.pallas import tpu as pltpu

F32 = jnp.float32
BF16 = jnp.bfloat16

D_MODEL = 1024
PLE_DIM = 256
EPS = 1e-6
ROPE_THETA = 10000.0
RET_HEADS = 4
RET_DK = 128
RET_DV = 256
RET_QK = RET_HEADS * RET_DK
RET_V = RET_HEADS * RET_DV
GDN_HEADS = 8
GDN_DK = 128
GDN_DV = 128
GDN_QK = GDN_HEADS * GDN_DK
GDN_V = GDN_HEADS * GDN_DV
GDN_CONV_DIM = 2 * GDN_QK + GDN_V
CONV_WIDTH = 4
D_FF = 2816
N_EXPERTS = 8
D_FF_EXPERT = 1408
OFF_GA = 2 * RET_QK + 2 * RET_V + GDN_CONV_DIM + GDN_V
OFF_MG = OFF_GA + 2 * GDN_HEADS
N_MAIN = OFF_GA + 2 * D_MODEL
LANES = 128
GDN_CHUNK = 64
RET_CHUNK = 256
MOE_TILE = 512
HALO = 8
VMEM_LIMIT = 56 * 1024 * 1024

_BLK_RV, _BLK_RG, _BLK_GQ, _BLK_GK, _BLK_GV, _BLK_GZ, _BLK_MGR, _BLK_MGG = 1, 2, 3, 4, 5, 6, 7, 8


def _silu(x):
    return x * jax.nn.sigmoid(x)


def _rms(x, g):
    return x * lax.rsqrt(jnp.mean(x * x, axis=-1, keepdims=True) + EPS) * g


def _dot(a, b):
    return jnp.dot(a, b, preferred_element_type=F32)


def _dot_nt(a, b):
    return lax.dot_general(a, b, (((1,), (1,)), ((), ())), preferred_element_type=F32)


def _params(*sem):
    return pltpu.CompilerParams(dimension_semantics=sem, vmem_limit_bytes=VMEM_LIMIT)


def _in_proj_kernel(x_ref, g_ref, wm_ref, wg_ref, om_ref, og_ref, *, tn):
    u = _rms(x_ref[...], g_ref[...]).astype(BF16)
    for j in range(N_MAIN // tn):
        om_ref[:, j * tn:(j + 1) * tn] = _dot(u, wm_ref[:, j * tn:(j + 1) * tn]).astype(BF16)
    og_ref[...] = _dot(u, wg_ref[...])


def _resident(shape):
    return pl.BlockSpec(shape, lambda *_: (0,) * len(shape), pipeline_mode=pl.Buffered(1))


def _in_proj(h, g, w_main, w_gab, tm=512, tn=1024):
    n = h.shape[0]
    return pl.pallas_call(
        functools.partial(_in_proj_kernel, tn=tn),
        grid=(n // tm,),
        in_specs=[
            pl.BlockSpec((tm, D_MODEL), lambda i: (i, 0)),
            _resident((1, D_MODEL)),
            _resident((D_MODEL, N_MAIN)),
            _resident((D_MODEL, LANES)),
        ],
        out_specs=[
            pl.BlockSpec((tm, N_MAIN), lambda i: (i, 0)),
            pl.BlockSpec((tm, LANES), lambda i: (i, 0)),
        ],
        out_shape=[
            jax.ShapeDtypeStruct((n, N_MAIN), BF16),
            jax.ShapeDtypeStruct((n, LANES), F32),
        ],
        compiler_params=_params("parallel"),
        name="in_proj",
    )(h, g, w_main, w_gab)


def _ret_kernel(q_ref, k_ref, v_ref, rg_ref, cos_ref, sin_ref, dmask_ref, qdec_ref, kdec_ref,
                gn_ref, o_ref, state_ref, *, chunk_decay):
    @pl.when(pl.program_id(1) == 0)
    def _():
        state_ref[...] = jnp.zeros_like(state_ref)

    cos = cos_ref[...]
    sin = sin_ref[...]
    for hd in range(RET_HEADS):
        qk_sl = slice(hd * RET_DK, (hd + 1) * RET_DK)
        v_sl = slice(hd * RET_DV, (hd + 1) * RET_DV)
        q = q_ref[:, qk_sl].astype(F32)
        k = k_ref[:, qk_sl].astype(F32)
        q = q * cos + pltpu.roll(q, RET_DK // 2, 1) * sin
        k = (k * cos + pltpu.roll(k, RET_DK // 2, 1) * sin) * (RET_DK ** -0.5)
        v = v_ref[:, v_sl]
        scores = _dot_nt(q.astype(BF16), k.astype(BF16)) * dmask_ref[hd]
        state = state_ref[hd]
        o = _dot(scores.astype(BF16), v) + _dot((q * qdec_ref[:, qk_sl]).astype(BF16), state.astype(BF16))
        k_dec_t = (k * kdec_ref[:, qk_sl]).T.astype(BF16)
        state_ref[hd] = state * chunk_decay[hd] + _dot(k_dec_t, v)
        mu = jnp.mean(o, axis=-1, keepdims=True)
        oc = o - mu
        var = jnp.mean(oc * oc, axis=-1, keepdims=True)
        o = oc * lax.rsqrt(var + EPS) * gn_ref[:, v_sl]
        o_ref[:, v_sl] = (o * _silu(rg_ref[:, v_sl].astype(F32))).astype(BF16)


def _retention(proj, cos_t, sin_t, ret_gn, batch, seq):
    c = RET_CHUNK
    nc = seq // c
    log_gamma = np.log1p(-np.exp2(-5.0 - np.arange(RET_HEADS, dtype=np.float64)))
    chunk_decay = tuple(float(np.exp(c * lg)) for lg in log_gamma)
    lg = jnp.asarray(log_gamma, F32)
    idx = jnp.arange(c, dtype=F32)
    rel = idx[:, None] - idx[None, :]
    causal = rel >= 0
    dmask = jnp.where(causal[None], jnp.exp(jnp.where(causal, rel, 0.0)[None] * lg[:, None, None]), 0.0)
    qdec = jnp.repeat(jnp.exp((idx + 1.0)[:, None] * lg[None, :]), RET_DK, axis=1)
    kdec = jnp.repeat(jnp.exp((c - 1.0 - idx)[:, None] * lg[None, :]), RET_DK, axis=1)
    row = lambda b, t: b * nc + t
    return pl.pallas_call(
        functools.partial(_ret_kernel, chunk_decay=chunk_decay),
        grid=(batch, nc),
        in_specs=[
            pl.BlockSpec((c, RET_QK), lambda b, t: (row(b, t), 0)),
            pl.BlockSpec((c, RET_QK), lambda b, t: (row(b, t), 1)),
            pl.BlockSpec((c, RET_V), lambda b, t: (row(b, t), _BLK_RV)),
            pl.BlockSpec((c, RET_V), lambda b, t: (row(b, t), _BLK_RG)),
            pl.BlockSpec((c, RET_DK), lambda b, t: (t, 0)),
            pl.BlockSpec((c, RET_DK), lambda b, t: (t, 0)),
            pl.BlockSpec((RET_HEADS, c, c), lambda b, t: (0, 0, 0)),
            pl.BlockSpec((c, RET_QK), lambda b, t: (0, 0)),
            pl.BlockSpec((c, RET_QK), lambda b, t: (0, 0)),
            pl.BlockSpec((1, RET_V), lambda b, t: (0, 0)),
        ],
        out_specs=pl.BlockSpec((c, RET_V), lambda b, t: (row(b, t), 0)),
        out_shape=jax.ShapeDtypeStruct((batch * seq, RET_V), BF16),
        scratch_shapes=[pltpu.VMEM((RET_HEADS, RET_DK, RET_DV), F32)],
        compiler_params=_params("parallel", "arbitrary"),
        name="retention",
    )(proj, proj, proj, proj, cos_t, sin_t, dmask, qdec, kdec, ret_gn)


def _gdn_prepare_pieces(slot, gq_ref, gk_ref, gv_ref, gab_ref, cw_ref, alog_ref, dtb_ref, buf_ref, qkv_ref,
                        gate_ref):
    c = GDN_CHUNK

    def column_block(j):
        src_ref = (gq_ref, gk_ref, gv_ref)[j // GDN_HEADS]
        src_sl = slice((j % GDN_HEADS) * LANES, (j % GDN_HEADS + 1) * LANES)
        sl = slice(j * LANES, (j + 1) * LANES)
        buf_ref[HALO:HALO + c, sl] = src_ref[:, src_sl].astype(F32)
        y = buf_ref[HALO:HALO + c, sl] * cw_ref[CONV_WIDTH - 1:CONV_WIDTH, sl]
        for s in range(1, CONV_WIDTH):
            y = y + buf_ref[HALO - s:HALO - s + c, sl] * cw_ref[CONV_WIDTH - 1 - s:CONV_WIDTH - s, sl]
        buf_ref[0:HALO, sl] = buf_ref[c:c + HALO, sl]
        y = _silu(y)
        if j < 2 * GDN_HEADS:
            scale = GDN_DK ** -0.5 if j < GDN_HEADS else 1.0
            y = y * (lax.rsqrt(jnp.sum(y * y, axis=-1, keepdims=True) + EPS) * scale)
        qkv_ref[slot, :, sl] = y

    def gates():
        gab = gab_ref[...]
        z = gab + dtb_ref[...]
        softplus = jnp.maximum(z, 0.0) + jnp.log(1.0 + jnp.exp(-jnp.abs(z)))
        gc = -jnp.exp(alog_ref[...]) * softplus
        row = lax.broadcasted_iota(jnp.int32, (c, LANES), 0)
        for s in (1, 2, 4, 8, 16, 32):
            gc = gc + jnp.where(row >= s, pltpu.roll(gc, s, 0), 0.0)
        gate_ref[slot, 0] = gc
        gate_ref[slot, 1] = jax.nn.sigmoid(gab)

    return [functools.partial(column_block, j) for j in range(GDN_CONV_DIM // LANES)] + [gates]


def _gdn_delta_rule_stages(slot, gz_ref, gn_ref, o_ref, state_ref, qkv_ref, gate_ref):
    c = GDN_CHUNK
    gc = gate_ref[slot, 0]
    beta = gate_ref[slot, 1]
    gc_t = gc.T
    ii = lax.broadcasted_iota(jnp.int32, (c, c), 0)
    jj = lax.broadcasted_iota(jnp.int32, (c, c), 1)
    incl = ii >= jj
    strict = ii > jj
    gn = gn_ref[...]
    hs = range(GDN_HEADS)
    sls = [slice(hd * GDN_DK, (hd + 1) * GDN_DK) for hd in hs]
    q_l = [qkv_ref[slot, :, sls[hd]] for hd in hs]
    k_l = [qkv_ref[slot, :, GDN_QK + hd * GDN_DK:GDN_QK + (hd + 1) * GDN_DK] for hd in hs]
    v_l = [qkv_ref[slot, :, 2 * GDN_QK + hd * GDN_DV:2 * GDN_QK + (hd + 1) * GDN_DV] for hd in hs]
    gcol_l = [gc[:, hd:hd + 1] for hd in hs]
    glast_l = [gc[c - 1:c, hd:hd + 1] for hd in hs]
    bcol_l = [beta[:, GDN_HEADS + hd:GDN_HEADS + hd + 1] for hd in hs]
    m_l, x_l, qk_l, ecol_l = [], [], [], []
    for hd in hs:
        decay = jnp.where(incl, jnp.exp(jnp.where(incl, gcol_l[hd] - gc_t[hd:hd + 1, :], 0.0)), 0.0)
        kb = k_l[hd] * bcol_l[hd]
        both = _dot_nt(jnp.concatenate([kb, q_l[hd]], axis=0).astype(BF16), k_l[hd].astype(BF16))
        m_l.append(-(both[0:c] * jnp.where(strict, decay, 0.0)))
        qk_l.append(both[c:2 * c] * decay)
        e_col = jnp.exp(gcol_l[hd])
        ecol_l.append(e_col)
        x_l.append(jnp.concatenate([v_l[hd] * bcol_l[hd], kb * e_col], axis=1))
    yield
    for it in range(6):
        mb_l = [m.astype(BF16) for m in m_l]
        x_l = [x_l[hd] + _dot(mb_l[hd], x_l[hd].astype(BF16)) for hd in hs]
        if it < 5:
            m_l = [_dot(mb_l[hd], mb_l[hd]) for hd in hs]
        yield
    st_l = [state_ref[hd] for hd in hs]
    ws_l = [_dot(jnp.concatenate([x_l[hd][:, GDN_DV:], q_l[hd] * ecol_l[hd]], axis=0).astype(BF16),
                 st_l[hd].astype(BF16)) for hd in hs]
    vn_l = [(x_l[hd][:, 0:GDN_DV] - ws_l[hd][0:c]).astype(BF16) for hd in hs]
    yield
    o_l = [ws_l[hd][c:2 * c] + _dot(qk_l[hd].astype(BF16), vn_l[hd]) for hd in hs]
    for hd in hs:
        k_dec_t = (k_l[hd] * jnp.exp(glast_l[hd] - gcol_l[hd])).T.astype(BF16)
        state_ref[hd] = st_l[hd] * jnp.exp(glast_l[hd]) + _dot(k_dec_t, vn_l[hd])
    yield
    for hd in hs:
        o = o_l[hd]
        o = o * lax.rsqrt(jnp.mean(o * o, axis=-1, keepdims=True) + EPS) * gn
        o_ref[:, sls[hd]] = (o * _silu(gz_ref[:, sls[hd]].astype(F32))).astype(BF16)


def _gdn_kernel(gq_ref, gk_ref, gv_ref, gz_ref, gab_ref, cw_ref, alog_ref, dtb_ref, gn_ref,
                o_ref, state_ref, buf_ref, qkv_ref, gate_ref):
    t = pl.program_id(1)

    @pl.when(t == 0)
    def _():
        state_ref[...] = jnp.zeros_like(state_ref)
        buf_ref[0:HALO, :] = jnp.zeros((HALO, GDN_CONV_DIM), F32)
        qkv_ref[...] = jnp.zeros_like(qkv_ref)
        gate_ref[...] = jnp.zeros_like(gate_ref)

    slot = t % 2
    pieces = _gdn_prepare_pieces(slot, gq_ref, gk_ref, gv_ref, gab_ref, cw_ref, alog_ref, dtb_ref, buf_ref,
                                 qkv_ref, gate_ref)
    per_stage = 3
    for _ in _gdn_delta_rule_stages(1 - slot, gz_ref, gn_ref, o_ref, state_ref, qkv_ref, gate_ref):
        for piece in pieces[:per_stage]:
            piece()
        pieces = pieces[per_stage:]
    for piece in pieces:
        piece()


def _gdn(proj, gab, conv_w, a_log, dt_bias, gdn_gn, batch, seq):
    c = GDN_CHUNK
    nc = seq // c
    row = lambda b, t: b * nc + jnp.minimum(t, nc - 1)
    prev = lambda b, t: b * nc + jnp.maximum(t - 1, 0)
    pad = lambda a: jnp.pad(a.reshape(1, -1), ((0, 0), (0, LANES - a.shape[-1])))
    return pl.pallas_call(
        _gdn_kernel,
        grid=(batch, nc + 1),
        in_specs=[
            pl.BlockSpec((c, GDN_QK), lambda b, t: (row(b, t), _BLK_GQ)),
            pl.BlockSpec((c, GDN_QK), lambda b, t: (row(b, t), _BLK_GK)),
            pl.BlockSpec((c, GDN_V), lambda b, t: (row(b, t), _BLK_GV)),
            pl.BlockSpec((c, GDN_V), lambda b, t: (prev(b, t), _BLK_GZ)),
            pl.BlockSpec((c, LANES), lambda b, t: (row(b, t), 0)),
            pl.BlockSpec((CONV_WIDTH, GDN_CONV_DIM), lambda b, t: (0, 0)),
            pl.BlockSpec((1, LANES), lambda b, t: (0, 0)),
            pl.BlockSpec((1, LANES), lambda b, t: (0, 0)),
            pl.BlockSpec((1, GDN_DV), lambda b, t: (0, 0)),
        ],
        out_specs=pl.BlockSpec((c, GDN_V), lambda b, t: (prev(b, t), 0)),
        out_shape=jax.ShapeDtypeStruct((batch * seq, GDN_V), BF16),
        scratch_shapes=[
            pltpu.VMEM((GDN_HEADS, GDN_DK, GDN_DV), F32),
            pltpu.VMEM((HALO + c, GDN_CONV_DIM), F32),
            pltpu.VMEM((2, c, GDN_CONV_DIM), F32),
            pltpu.VMEM((2, 2, c, LANES), F32),
        ],
        compiler_params=_params("parallel", "arbitrary"),
        name="gdn",
    )(proj, proj, proj, proj, gab, conv_w, pad(a_log), pad(dt_bias), gdn_gn.reshape(1, -1))


def _mix_out_kernel(h_ref, oret_ref, ogdn_ref, mgr_ref, mgg_ref, wbr_ref, wbg_ref, wo_ref, out_ref):
    y_ret = _dot(oret_ref[...], wbr_ref[...])
    y_gdn = _dot(ogdn_ref[...], wbg_ref[...])
    merged = (jax.nn.sigmoid(mgr_ref[...].astype(F32)) * y_ret
              + jax.nn.sigmoid(mgg_ref[...].astype(F32)) * y_gdn)
    out_ref[...] = h_ref[...] + _dot(merged.astype(BF16), wo_ref[...])


def _mix_out(h, o_ret, o_gdn, proj, w_br_ret, w_br_gdn, w_out, tm=512):
    n = h.shape[0]
    tile = lambda blk: pl.BlockSpec((tm, D_MODEL), lambda i: (i, blk))
    weight = pl.BlockSpec((D_MODEL, D_MODEL), lambda i: (0, 0))
    return pl.pallas_call(
        _mix_out_kernel,
        grid=(n // tm,),
        in_specs=[tile(0), tile(0), tile(0), tile(_BLK_MGR), tile(_BLK_MGG), weight, weight, weight],
        out_specs=tile(0),
        out_shape=jax.ShapeDtypeStruct((n, D_MODEL), F32),
        compiler_params=_params("parallel"),
        name="mix_out",
    )(h, o_ret, o_gdn, proj, proj, w_br_ret, w_br_gdn, w_out)


def _ple_tail(h1, p_ref, gple_ref, wple_ref, wpg_ref, gfin_ref, out_ref, final):
    ple = _dot(p_ref[...].astype(BF16), wple_ref[...])
    gate = jax.nn.sigmoid(_dot(_rms(h1, gple_ref[...]).astype(BF16), wpg_ref[...]))
    h2 = h1 + gate * ple
    out_ref[...] = _rms(h2, gfin_ref[...]) if final else h2


def _ffn_kernel(h_ref, g_ref, wg_ref, wu_ref, wd_ref, p_ref, gple_ref, wple_ref, wpg_ref, gfin_ref,
                out_ref, *, final, tf):
    h = h_ref[...]
    u = _rms(h, g_ref[...]).astype(BF16)
    for j in range(D_FF // tf):
        cols = slice(j * tf, (j + 1) * tf)
        act = _silu(_dot(u, wg_ref[:, cols])) * _dot(u, wu_ref[:, cols])
        h = h + _dot(act.astype(BF16), wd_ref[cols, :])
    _ple_tail(h, p_ref, gple_ref, wple_ref, wpg_ref, gfin_ref, out_ref, final)


def _ffn(h, g, wg, wu, wd, p, g_ple, w_ple, w_ple_gate, g_final, final, tm=512, tf=D_FF // 2):
    n = h.shape[0]
    return pl.pallas_call(
        functools.partial(_ffn_kernel, final=final, tf=tf),
        grid=(n // tm,),
        in_specs=[
            pl.BlockSpec((tm, D_MODEL), lambda i: (i, 0)),
            _resident((1, D_MODEL)),
            _resident((D_MODEL, D_FF)),
            _resident((D_MODEL, D_FF)),
            _resident((D_FF, D_MODEL)),
            pl.BlockSpec((tm, PLE_DIM), lambda i: (i, 0)),
            _resident((1, D_MODEL)),
            _resident((PLE_DIM, D_MODEL)),
            _resident((D_MODEL, D_MODEL)),
            _resident((1, D_MODEL)),
        ],
        out_specs=pl.BlockSpec((tm, D_MODEL), lambda i: (i, 0)),
        out_shape=jax.ShapeDtypeStruct((n, D_MODEL), F32),
        compiler_params=_params("parallel"),
        name="ffn_ple",
    )(h, g, wg, wu, wd, p, g_ple, w_ple, w_ple_gate, g_final)


def _router_kernel(h_ref, g_ref, wr_ref, route_ref):
    u = _rms(h_ref[...], g_ref[...])
    logits = jnp.dot(u, wr_ref[...], preferred_element_type=F32, precision=lax.Precision.HIGHEST)
    lane = lax.broadcasted_iota(jnp.int32, logits.shape, 1)
    logits = jnp.where(lane < N_EXPERTS, logits, -jnp.inf)
    m1 = jnp.max(logits, axis=-1, keepdims=True)
    i1 = jnp.min(jnp.where(logits == m1, lane, LANES), axis=-1, keepdims=True)
    rest = jnp.where(lane == i1, -jnp.inf, logits)
    m2 = jnp.max(rest, axis=-1, keepdims=True)
    i2 = jnp.min(jnp.where(rest == m2, lane, LANES), axis=-1, keepdims=True)
    e2 = jnp.exp(m2 - m1)
    w1 = 1.0 / (1.0 + e2)
    route_ref[...] = (jnp.where(lane == 0, i1.astype(F32), 0.0) + jnp.where(lane == 1, i2.astype(F32), 0.0)
                      + jnp.where(lane == 2, w1, 0.0) + jnp.where(lane == 3, e2 * w1, 0.0))


def _router(h, g, w_router, tm=1024):
    n = h.shape[0]
    return pl.pallas_call(
        _router_kernel,
        grid=(n // tm,),
        in_specs=[
            pl.BlockSpec((tm, D_MODEL), lambda i: (i, 0)),
            pl.BlockSpec((1, D_MODEL), lambda i: (0, 0)),
            pl.BlockSpec((D_MODEL, LANES), lambda i: (0, 0)),
        ],
        out_specs=pl.BlockSpec((tm, LANES), lambda i: (i, 0)),
        out_shape=jax.ShapeDtypeStruct((n, LANES), F32),
        compiler_params=_params("parallel"),
        name="router",
    )(h, g, w_router)


def _routing_tables(route, n):
    n_pairs = 2 * n
    fill_bit = 1 << 15
    assert n_pairs <= fill_bit and MOE_TILE <= fill_bit
    expert = route[:, 0:2].astype(jnp.int32).reshape(-1)
    ids = jnp.arange(N_EXPERTS, dtype=jnp.int32)
    counts = jnp.sum((expert[:, None] == ids[None, :]).astype(jnp.int32), axis=0)
    padded = ((counts + MOE_TILE - 1) // MOE_TILE) * MOE_TILE
    keys = (expert << 16) | jnp.arange(n_pairs, dtype=jnp.int32)
    fill_idx = jnp.arange(MOE_TILE, dtype=jnp.int32)[None, :]
    fill_keys = jnp.where(fill_idx < (padded - counts)[:, None], (ids[:, None] << 16) | fill_bit | fill_idx,
                          jnp.iinfo(jnp.int32).max)
    keys = jnp.sort(jnp.concatenate([keys, fill_keys.reshape(-1)]))
    keys = jnp.concatenate([keys, jnp.full((MOE_TILE,), jnp.iinfo(jnp.int32).max, jnp.int32)])
    low = keys & (2 * fill_bit - 1)
    valid = low < fill_bit
    row_expert = jnp.minimum(keys >> 16, N_EXPERTS - 1)
    spare = n_pairs + row_expert * MOE_TILE + jnp.minimum(low - fill_bit, MOE_TILE - 1)
    src_row = jnp.where(valid, low >> 1, 0)
    dst_row = jnp.where(valid, (low & 1) * n + (low >> 1), spare)
    tile_expert = row_expert[::MOE_TILE]
    n_used = (jnp.sum(padded) // MOE_TILE).reshape(1)
    return src_row, dst_row, tile_expert, n_used


def _rows_start(count, src_of, dst_of, sem, unrolled):
    def one(r):
        pltpu.make_async_copy(src_of(r), dst_of(r), sem).start()
    if unrolled:
        for r in range(count):
            one(r)
    else:
        def body(r, carry):
            one(r)
            return carry
        lax.fori_loop(0, count, body, 0, unroll=8)


def _moe_gemm_kernel(src_row_ref, dst_row_ref, tile_expert_ref, n_used_ref, h_hbm, g_ref, wg_ref, wu_ref,
                     wd_ref, y_hbm, xs_ref, ys_ref, gsem, ssem):
    del tile_expert_ref
    i = pl.program_id(0)
    n_used = n_used_ref[0]
    rows = pl.ds(0, MOE_TILE)

    def gather_start(tile, unrolled):
        _rows_start(MOE_TILE, lambda r: h_hbm.at[pl.ds(src_row_ref[tile * MOE_TILE + r], 1)],
                    lambda r: xs_ref.at[tile % 2, pl.ds(r, 1)], gsem.at[tile % 2], unrolled)

    def gather_wait(tile):
        pltpu.make_async_copy(h_hbm.at[rows], xs_ref.at[tile % 2], gsem.at[tile % 2]).wait()

    def scatter_start(tile, unrolled):
        _rows_start(MOE_TILE, lambda r: ys_ref.at[tile % 2, pl.ds(r, 1)],
                    lambda r: y_hbm.at[pl.ds(dst_row_ref[tile * MOE_TILE + r], 1)], ssem.at[tile % 2], unrolled)

    def scatter_wait(tile):
        pltpu.make_async_copy(ys_ref.at[tile % 2], y_hbm.at[rows], ssem.at[tile % 2]).wait()

    def compute(tile):
        u = _rms(xs_ref[tile % 2], g_ref[...]).astype(BF16)
        act = _silu(_dot(u, wg_ref[0])) * _dot(u, wu_ref[0])
        ys_ref[tile % 2] = _dot(act.astype(BF16), wd_ref[0])

    @pl.when(i == 0)
    def _():
        gather_start(0, False)
        gather_start(1, False)
        ys_ref[1] = jnp.zeros((MOE_TILE, D_MODEL), F32)
        n_pairs = y_hbm.shape[0] - N_EXPERTS * MOE_TILE
        for e in range(N_EXPERTS):
            block = pltpu.make_async_copy(ys_ref.at[1], y_hbm.at[pl.ds(n_pairs + e * MOE_TILE, MOE_TILE)],
                                          ssem.at[1])
            block.start()
            block.wait()
        gather_wait(0)
        compute(0)

    @pl.when(jnp.logical_and(i >= 1, i < n_used))
    def _():
        gather_wait(i)
        gather_start(i + 1, True)
        scatter_start(i - 1, True)
        compute(i)
        scatter_wait(i - 1)

    @pl.when(i == n_used - 1)
    def _():
        gather_wait(i + 1)
        scatter_start(i, False)
        scatter_wait(i)


def _moe_gemm(h, g, src_row, dst_row, tile_expert, n_used, wg, wu, wd):
    n = h.shape[0]
    n_tiles = src_row.shape[0] // MOE_TILE - 1
    expert_block = lambda i, sr, dr, te, nu: (te[jnp.minimum(i, nu[0] - 1)], 0, 0)
    return pl.pallas_call(
        _moe_gemm_kernel,
        grid_spec=pltpu.PrefetchScalarGridSpec(
            num_scalar_prefetch=4,
            grid=(n_tiles,),
            in_specs=[
                pl.BlockSpec(memory_space=pl.ANY),
                pl.BlockSpec((1, D_MODEL), lambda i, sr, dr, te, nu: (0, 0)),
                pl.BlockSpec((1, D_MODEL, D_FF_EXPERT), expert_block),
                pl.BlockSpec((1, D_MODEL, D_FF_EXPERT), expert_block),
                pl.BlockSpec((1, D_FF_EXPERT, D_MODEL), expert_block),
            ],
            out_specs=pl.BlockSpec(memory_space=pl.ANY),
            scratch_shapes=[
                pltpu.VMEM((2, MOE_TILE, D_MODEL), F32),
                pltpu.VMEM((2, MOE_TILE, D_MODEL), F32),
                pltpu.SemaphoreType.DMA((2,)),
                pltpu.SemaphoreType.DMA((2,)),
            ],
        ),
        out_shape=jax.ShapeDtypeStruct((2 * n + N_EXPERTS * MOE_TILE, D_MODEL), F32),
        compiler_params=_params("arbitrary"),
        name="moe_gemm",
    )(src_row, dst_row, tile_expert, n_used, h, g, wg, wu, wd)


def _moe_tail_kernel(h_ref, route_ref, y0_ref, y1_ref, p_ref, gple_ref, wple_ref, wpg_ref, gfin_ref, out_ref,
                     *, final):
    route = route_ref[...]
    h1 = h_ref[...] + route[:, 2:3] * y0_ref[...] + route[:, 3:4] * y1_ref[...]
    _ple_tail(h1, p_ref, gple_ref, wple_ref, wpg_ref, gfin_ref, out_ref, final)


def _moe_tail(h, route, y, p, g_ple, w_ple, w_ple_gate, g_final, final, tm=512):
    n = h.shape[0]
    return pl.pallas_call(
        functools.partial(_moe_tail_kernel, final=final),
        grid=(n // tm,),
        in_specs=[
            pl.BlockSpec((tm, D_MODEL), lambda i: (i, 0)),
            pl.BlockSpec((tm, LANES), lambda i: (i, 0)),
            pl.BlockSpec((tm, D_MODEL), lambda i: (i, 0)),
            pl.BlockSpec((tm, D_MODEL), lambda i: (n // tm + i, 0)),
            pl.BlockSpec((tm, PLE_DIM), lambda i: (i, 0)),
            _resident((1, D_MODEL)),
            _resident((PLE_DIM, D_MODEL)),
            _resident((D_MODEL, D_MODEL)),
            _resident((1, D_MODEL)),
        ],
        out_specs=pl.BlockSpec((tm, D_MODEL), lambda i: (i, 0)),
        out_shape=jax.ShapeDtypeStruct((n, D_MODEL), F32),
        compiler_params=_params("parallel"),
        name="moe_tail",
    )(h, route, y, y, p, g_ple, w_ple, w_ple_gate, g_final)


def _moe(h, g, w_router, wg, wu, wd, p, g_ple, w_ple, w_ple_gate, g_final, final):
    n = h.shape[0]
    route = _router(h, g, w_router)
    src_row, dst_row, tile_expert, n_used = _routing_tables(route, n)
    y = _moe_gemm(h, g, src_row, dst_row, tile_expert, n_used, wg, wu, wd)
    return _moe_tail(h, route, y, p, g_ple, w_ple, w_ple_gate, g_final, final)


def _rotary_tables(seq):
    half = RET_DK // 2
    pos = jnp.arange(seq, dtype=F32)
    inv_freq = ROPE_THETA ** (-jnp.arange(half, dtype=F32) / half)
    ang = pos[:, None] * inv_freq[None, :]
    cos, sin = jnp.cos(ang), jnp.sin(ang)
    return jnp.concatenate([cos, cos], axis=-1), jnp.concatenate([-sin, sin], axis=-1)


def kernel(x, p, w_in, conv_w, a_log, dt_bias, ret_gn, gdn_gn, w_br_ret, w_br_gdn, w_out, norm_mix,
           norm_ffn, norm_ple, w_ple, w_ple_gate, ffn_w_gate, ffn_w_up, ffn_w_down, router,
           exp_w_gate, exp_w_up, exp_w_down, norm_final):
    batch, seq, _ = x.shape
    depth = w_in.shape[0]
    n = batch * seq
    cos_t, sin_t = _rotary_tables(seq)
    g_final = norm_final.reshape(1, -1)
    h = x.reshape(n, D_MODEL)
    for i in range(depth):
        w_main = jnp.concatenate([w_in[i, :, :OFF_GA], w_in[i, :, OFF_MG:]], axis=1).astype(BF16)
        w_gab = jnp.pad(w_in[i, :, OFF_GA:OFF_MG], ((0, 0), (0, LANES - 2 * GDN_HEADS))).astype(BF16)
        proj, gab = _in_proj(h, norm_mix[i].reshape(1, -1), w_main, w_gab)
        o_ret = _retention(proj, cos_t, sin_t, ret_gn[i].reshape(1, -1), batch, seq)
        o_gdn = _gdn(proj, gab, conv_w[i], a_log[i], dt_bias[i], gdn_gn[i], batch, seq)
        h = _mix_out(h, o_ret, o_gdn, proj, w_br_ret[i].astype(BF16), w_br_gdn[i].astype(BF16),
                     w_out[i].astype(BF16))
        final = i == depth - 1
        tail = (p[i].reshape(n, PLE_DIM), norm_ple[i].reshape(1, -1), w_ple[i].astype(BF16),
                w_ple_gate[i].astype(BF16), g_final, final)
        j = i // 2
        g_ffn = norm_ffn[i].reshape(1, -1)
        if i % 2 == 0:
            h = _ffn(h, g_ffn, ffn_w_gate[j].astype(BF16), ffn_w_up[j].astype(BF16),
                     ffn_w_down[j].astype(BF16), *tail)
        else:
            w_router = jnp.pad(router[j], ((0, 0), (0, LANES - N_EXPERTS)))
            h = _moe(h, g_ffn, w_router, exp_w_gate[j].astype(BF16), exp_w_up[j].astype(BF16),
                     exp_w_down[j].astype(BF16), *tail)
    return h.reshape(batch, seq, D_MODEL)
```

```python
import functools

import numpy as np
import jax
import jax.numpy as jnp
from jax import lax
from jax.experimental import pallas as pl
from jax.experimental.pallas import tpu as pltpu

F32 = jnp.float32
BF16 = jnp.bfloat16

D_MODEL = 1024
PLE_DIM = 256
EPS = 1e-6
ROPE_THETA = 10000.0
RET_HEADS = 4
RET_DK = 128
RET_DV = 256
RET_QK = RET_HEADS * RET_DK
RET_V = RET_HEADS * RET_DV
GDN_HEADS = 8
GDN_DK = 128
GDN_DV = 128
GDN_QK = GDN_HEADS * GDN_DK
GDN_V = GDN_HEADS * GDN_DV
GDN_CONV_DIM = 2 * GDN_QK + GDN_V
CONV_WIDTH = 4
D_FF = 2816
N_EXPERTS = 8
D_FF_EXPERT = 1408
OFF_GA = 2 * RET_QK + 2 * RET_V + GDN_CONV_DIM + GDN_V
OFF_MG = OFF_GA + 2 * GDN_HEADS
N_MAIN = OFF_GA + 2 * D_MODEL
LANES = 128
GDN_CHUNK = 64
RET_CHUNK = 256
MOE_TILE = 512
MOE_STAGES = 3
BF16_TILE = 16
VMEM_LIMIT = 56 * 1024 * 1024

_BLK_RV, _BLK_RG, _BLK_GQ, _BLK_GK, _BLK_GV, _BLK_GZ, _BLK_MGR, _BLK_MGG = 1, 2, 3, 4, 5, 6, 7, 8


def _silu(x):
    return x * jax.nn.sigmoid(x)


def _rms(x, g):
    return x * lax.rsqrt(jnp.mean(x * x, axis=-1, keepdims=True) + EPS) * g


def _dot(a, b):
    return jnp.dot(a, b, preferred_element_type=F32)


def _dot_nt(a, b):
    return lax.dot_general(a, b, (((1,), (1,)), ((), ())), preferred_element_type=F32)


def _params(*sem):
    return pltpu.CompilerParams(dimension_semantics=sem, vmem_limit_bytes=VMEM_LIMIT)


def _in_proj_kernel(x_ref, g_ref, wa_ref, wb_ref, wg_ref, om_ref, og_ref, *, tn):
    u = _rms(x_ref[...], g_ref[...]).astype(BF16)
    for j in range(N_MAIN // tn):
        w_ref, k = (wa_ref, j) if j < OFF_GA // tn else (wb_ref, j - OFF_GA // tn)
        om_ref[:, j * tn:(j + 1) * tn] = _dot(u, w_ref[:, k * tn:(k + 1) * tn]).astype(BF16)
    og_ref[...] = _dot(u, wg_ref[...])


def _resident(shape):
    return pl.BlockSpec(shape, lambda *_: (0,) * len(shape), pipeline_mode=pl.Buffered(1))


def _in_proj(h, g, w_a, w_b, w_gab, tm=512, tn=1024):
    n = h.shape[0]
    return pl.pallas_call(
        functools.partial(_in_proj_kernel, tn=tn),
        grid=(n // tm,),
        in_specs=[
            pl.BlockSpec((tm, D_MODEL), lambda i: (i, 0)),
            _resident((1, D_MODEL)),
            _resident((D_MODEL, OFF_GA)),
            _resident((D_MODEL, N_MAIN - OFF_GA)),
            _resident((D_MODEL, LANES)),
        ],
        out_specs=[
            pl.BlockSpec((tm, N_MAIN), lambda i: (i, 0)),
            pl.BlockSpec((tm, LANES), lambda i: (i, 0)),
        ],
        out_shape=[
            jax.ShapeDtypeStruct((n, N_MAIN), BF16),
            jax.ShapeDtypeStruct((n, LANES), F32),
        ],
        compiler_params=_params("parallel"),
        name="in_proj",
    )(h, g, w_a, w_b, w_gab)


def _ret_kernel(q_ref, k_ref, v_ref, rg_ref, cos_ref, sin_ref, dmask_ref, qdec_ref, kdec_ref,
                gn_ref, o_ref, state_ref, *, chunk_decay):
    @pl.when(pl.program_id(1) == 0)
    def _():
        state_ref[...] = jnp.zeros_like(state_ref)

    cos = cos_ref[...]
    sin = sin_ref[...]
    for hd in range(RET_HEADS):
        qk_sl = slice(hd * RET_DK, (hd + 1) * RET_DK)
        v_sl = slice(hd * RET_DV, (hd + 1) * RET_DV)
        q = q_ref[:, qk_sl].astype(F32)
        k = k_ref[:, qk_sl].astype(F32)
        q = q * cos + pltpu.roll(q, RET_DK // 2, 1) * sin
        k = (k * cos + pltpu.roll(k, RET_DK // 2, 1) * sin) * (RET_DK ** -0.5)
        v = v_ref[:, v_sl]
        scores = _dot_nt(q.astype(BF16), k.astype(BF16)) * dmask_ref[hd]
        state = state_ref[hd]
        o = _dot(scores.astype(BF16), v) + _dot((q * qdec_ref[:, qk_sl]).astype(BF16), state.astype(BF16))
        k_dec_t = (k * kdec_ref[:, qk_sl]).T.astype(BF16)
        state_ref[hd] = state * chunk_decay[hd] + _dot(k_dec_t, v)
        mu = jnp.mean(o, axis=-1, keepdims=True)
        oc = o - mu
        var = jnp.mean(oc * oc, axis=-1, keepdims=True)
        o = oc * lax.rsqrt(var + EPS) * gn_ref[:, v_sl]
        o_ref[:, v_sl] = (o * _silu(rg_ref[:, v_sl].astype(F32))).astype(BF16)


def _retention(proj, cos_t, sin_t, ret_gn, batch, seq):
    c = RET_CHUNK
    nc = seq // c
    log_gamma = np.log1p(-np.exp2(-5.0 - np.arange(RET_HEADS, dtype=np.float64)))
    chunk_decay = tuple(float(np.exp(c * lg)) for lg in log_gamma)
    lg = jnp.asarray(log_gamma, F32)
    idx = jnp.arange(c, dtype=F32)
    rel = idx[:, None] - idx[None, :]
    causal = rel >= 0
    dmask = jnp.where(causal[None], jnp.exp(jnp.where(causal, rel, 0.0)[None] * lg[:, None, None]), 0.0)
    qdec = jnp.repeat(jnp.exp((idx + 1.0)[:, None] * lg[None, :]), RET_DK, axis=1)
    kdec = jnp.repeat(jnp.exp((c - 1.0 - idx)[:, None] * lg[None, :]), RET_DK, axis=1)
    row = lambda b, t: b * nc + t
    return pl.pallas_call(
        functools.partial(_ret_kernel, chunk_decay=chunk_decay),
        grid=(batch, nc),
        in_specs=[
            pl.BlockSpec((c, RET_QK), lambda b, t: (row(b, t), 0)),
            pl.BlockSpec((c, RET_QK), lambda b, t: (row(b, t), 1)),
            pl.BlockSpec((c, RET_V), lambda b, t: (row(b, t), _BLK_RV)),
            pl.BlockSpec((c, RET_V), lambda b, t: (row(b, t), _BLK_RG)),
            pl.BlockSpec((c, RET_DK), lambda b, t: (t, 0)),
            pl.BlockSpec((c, RET_DK), lambda b, t: (t, 0)),
            pl.BlockSpec((RET_HEADS, c, c), lambda b, t: (0, 0, 0)),
            pl.BlockSpec((c, RET_QK), lambda b, t: (0, 0)),
            pl.BlockSpec((c, RET_QK), lambda b, t: (0, 0)),
            pl.BlockSpec((1, RET_V), lambda b, t: (0, 0)),
        ],
        out_specs=pl.BlockSpec((c, RET_V), lambda b, t: (row(b, t), 0)),
        out_shape=jax.ShapeDtypeStruct((batch * seq, RET_V), BF16),
        scratch_shapes=[pltpu.VMEM((RET_HEADS, RET_DK, RET_DV), F32)],
        compiler_params=_params("parallel", "arbitrary"),
        name="retention",
    )(proj, proj, proj, proj, cos_t, sin_t, dmask, qdec, kdec, ret_gn)


def _gdn_prepare_pieces(slot, gq_ref, gk_ref, gv_ref, gab_ref, cw_ref, shift_ref, alog_ref, dtb_ref, buf_ref,
                        qkv_ref, gate_ref):
    c = GDN_CHUNK

    def column_block(j):
        width = 2 * LANES
        src_ref = (gq_ref, gk_ref, gv_ref)[j * width // GDN_QK]
        src_sl = slice(j * width % GDN_QK, j * width % GDN_QK + width)
        sl = slice(j * width, (j + 1) * width)
        buf_ref[BF16_TILE:BF16_TILE + c, sl] = src_ref[:, src_sl]
        shifted = _dot(shift_ref[...], buf_ref[:, sl])
        buf_ref[0:BF16_TILE, sl] = buf_ref[c:c + BF16_TILE, sl]
        y = shifted[0:c] * cw_ref[CONV_WIDTH - 1:CONV_WIDTH, sl]
        for s in range(1, CONV_WIDTH):
            y = y + shifted[s * c:(s + 1) * c] * cw_ref[CONV_WIDTH - 1 - s:CONV_WIDTH - s, sl]
        y = _silu(y)
        for half in range(2):
            hsl = slice(half * LANES, (half + 1) * LANES)
            yh = y[:, hsl]
            if sl.start < 2 * GDN_QK:
                scale = GDN_DK ** -0.5 if sl.start < GDN_QK else 1.0
                yh = yh * (lax.rsqrt(jnp.sum(yh * yh, axis=-1, keepdims=True) + EPS) * scale)
            qkv_ref[slot, :, sl.start + half * LANES:sl.start + (half + 1) * LANES] = yh

    def gates():
        gab = gab_ref[...]
        z = gab + dtb_ref[...]
        softplus = jnp.maximum(z, 0.0) + jnp.log(1.0 + jnp.exp(-jnp.abs(z)))
        gc = -jnp.exp(alog_ref[...]) * softplus
        row = lax.broadcasted_iota(jnp.int32, (c, LANES), 0)
        for s in (1, 2, 4, 8, 16, 32):
            gc = gc + jnp.where(row >= s, pltpu.roll(gc, s, 0), 0.0)
        gate_ref[slot, 0] = gc
        gate_ref[slot, 1] = jax.nn.sigmoid(gab)

    return [functools.partial(column_block, j) for j in range(GDN_CONV_DIM // (2 * LANES))] + [gates]


def _gdn_delta_rule_stages(slot, gz_ref, gn_ref, o_ref, state_ref, qkv_ref, gate_ref):
    c = GDN_CHUNK
    gc = gate_ref[slot, 0]
    beta = gate_ref[slot, 1]
    gc_t = gc.T
    ii = lax.broadcasted_iota(jnp.int32, (c, c), 0)
    jj = lax.broadcasted_iota(jnp.int32, (c, c), 1)
    incl = ii >= jj
    strict = ii > jj
    gn = gn_ref[...]
    hs = range(GDN_HEADS)
    sls = [slice(hd * GDN_DK, (hd + 1) * GDN_DK) for hd in hs]
    q_l = [qkv_ref[slot, :, sls[hd]] for hd in hs]
    k_l = [qkv_ref[slot, :, GDN_QK + hd * GDN_DK:GDN_QK + (hd + 1) * GDN_DK] for hd in hs]
    v_l = [qkv_ref[slot, :, 2 * GDN_QK + hd * GDN_DV:2 * GDN_QK + (hd + 1) * GDN_DV] for hd in hs]
    gcol_l = [gc[:, hd:hd + 1] for hd in hs]
    glast_l = [gc[c - 1:c, hd:hd + 1] for hd in hs]
    bcol_l = [beta[:, GDN_HEADS + hd:GDN_HEADS + hd + 1] for hd in hs]
    m_l, x_l, qk_l, ecol_l = [], [], [], []
    for hd in hs:
        decay = jnp.where(incl, jnp.exp(jnp.where(incl, gcol_l[hd] - gc_t[hd:hd + 1, :], 0.0)), 0.0)
        kb = k_l[hd] * bcol_l[hd]
        both = _dot_nt(jnp.concatenate([kb, q_l[hd]], axis=0).astype(BF16), k_l[hd].astype(BF16))
        m_l.append(-(both[0:c] * jnp.where(strict, decay, 0.0)))
        qk_l.append(both[c:2 * c] * decay)
        e_col = jnp.exp(gcol_l[hd])
        ecol_l.append(e_col)
        x_l.append(jnp.concatenate([v_l[hd] * bcol_l[hd], kb * e_col], axis=1))
    yield
    for it in range(6):
        mb_l = [m.astype(BF16) for m in m_l]
        x_l = [x_l[hd] + _dot(mb_l[hd], x_l[hd].astype(BF16)) for hd in hs]
        if it < 5:
            m_l = [_dot(mb_l[hd], mb_l[hd]) for hd in hs]
        yield
    st_l = [state_ref[hd] for hd in hs]
    ws_l = [_dot(jnp.concatenate([x_l[hd][:, GDN_DV:], q_l[hd] * ecol_l[hd]], axis=0).astype(BF16),
                 st_l[hd].astype(BF16)) for hd in hs]
    vn_l = [(x_l[hd][:, 0:GDN_DV] - ws_l[hd][0:c]).astype(BF16) for hd in hs]
    yield
    o_l = [ws_l[hd][c:2 * c] + _dot(qk_l[hd].astype(BF16), vn_l[hd]) for hd in hs]
    for hd in hs:
        k_dec_t = (k_l[hd] * jnp.exp(glast_l[hd] - gcol_l[hd])).T.astype(BF16)
        state_ref[hd] = st_l[hd] * jnp.exp(glast_l[hd]) + _dot(k_dec_t, vn_l[hd])
    yield
    for hd in hs:
        o = o_l[hd]
        o = o * lax.rsqrt(jnp.mean(o * o, axis=-1, keepdims=True) + EPS) * gn
        o_ref[:, sls[hd]] = (o * _silu(gz_ref[:, sls[hd]].astype(F32))).astype(BF16)


def _gdn_kernel(gq_ref, gk_ref, gv_ref, gz_ref, gab_ref, cw_ref, shift_ref, alog_ref, dtb_ref, gn_ref,
                o_ref, state_ref, buf_ref, qkv_ref, gate_ref):
    t = pl.program_id(1)

    @pl.when(t == 0)
    def _():
        state_ref[...] = jnp.zeros_like(state_ref)
        buf_ref[0:BF16_TILE, :] = jnp.zeros((BF16_TILE, GDN_CONV_DIM), BF16)
        qkv_ref[...] = jnp.zeros_like(qkv_ref)
        gate_ref[...] = jnp.zeros_like(gate_ref)

    slot = t % 2
    pieces = _gdn_prepare_pieces(slot, gq_ref, gk_ref, gv_ref, gab_ref, cw_ref, shift_ref, alog_ref, dtb_ref,
                                 buf_ref, qkv_ref, gate_ref)
    per_stage = 2
    for _ in _gdn_delta_rule_stages(1 - slot, gz_ref, gn_ref, o_ref, state_ref, qkv_ref, gate_ref):
        for piece in pieces[:per_stage]:
            piece()
        pieces = pieces[per_stage:]
    for piece in pieces:
        piece()


def _gdn(proj, gab, conv_w, a_log, dt_bias, gdn_gn, batch, seq):
    c = GDN_CHUNK
    nc = seq // c
    row = lambda b, t: b * nc + jnp.minimum(t, nc - 1)
    prev = lambda b, t: b * nc + jnp.maximum(t - 1, 0)
    pad = lambda a: jnp.pad(a.reshape(1, -1), ((0, 0), (0, LANES - a.shape[-1])))
    shift = np.zeros((CONV_WIDTH * c, BF16_TILE + c), np.float32)
    for s in range(CONV_WIDTH):
        shift[s * c + np.arange(c), BF16_TILE + np.arange(c) - s] = 1.0
    return pl.pallas_call(
        _gdn_kernel,
        grid=(batch, nc + 1),
        in_specs=[
            pl.BlockSpec((c, GDN_QK), lambda b, t: (row(b, t), _BLK_GQ)),
            pl.BlockSpec((c, GDN_QK), lambda b, t: (row(b, t), _BLK_GK)),
            pl.BlockSpec((c, GDN_V), lambda b, t: (row(b, t), _BLK_GV)),
            pl.BlockSpec((c, GDN_V), lambda b, t: (prev(b, t), _BLK_GZ)),
            pl.BlockSpec((c, LANES), lambda b, t: (row(b, t), 0)),
            pl.BlockSpec((CONV_WIDTH, GDN_CONV_DIM), lambda b, t: (0, 0)),
            pl.BlockSpec((CONV_WIDTH * c, BF16_TILE + c), lambda b, t: (0, 0)),
            pl.BlockSpec((1, LANES), lambda b, t: (0, 0)),
            pl.BlockSpec((1, LANES), lambda b, t: (0, 0)),
            pl.BlockSpec((1, GDN_DV), lambda b, t: (0, 0)),
        ],
        out_specs=pl.BlockSpec((c, GDN_V), lambda b, t: (prev(b, t), 0)),
        out_shape=jax.ShapeDtypeStruct((batch * seq, GDN_V), BF16),
        scratch_shapes=[
            pltpu.VMEM((GDN_HEADS, GDN_DK, GDN_DV), F32),
            pltpu.VMEM((BF16_TILE + c, GDN_CONV_DIM), BF16),
            pltpu.VMEM((2, c, GDN_CONV_DIM), F32),
            pltpu.VMEM((2, 2, c, LANES), F32),
        ],
        compiler_params=_params("parallel", "arbitrary"),
        name="gdn",
    )(proj, proj, proj, proj, gab, conv_w, jnp.asarray(shift, BF16), pad(a_log), pad(dt_bias), gdn_gn.reshape(1, -1))


def _mix_out_kernel(h_ref, oret_ref, ogdn_ref, mgr_ref, mgg_ref, wbr_ref, wbg_ref, wo_ref, out_ref):
    y_ret = _dot(oret_ref[...], wbr_ref[...])
    y_gdn = _dot(ogdn_ref[...], wbg_ref[...])
    merged = (jax.nn.sigmoid(mgr_ref[...].astype(F32)) * y_ret
              + jax.nn.sigmoid(mgg_ref[...].astype(F32)) * y_gdn)
    out_ref[...] = h_ref[...] + _dot(merged.astype(BF16), wo_ref[...])


def _mix_out(h, o_ret, o_gdn, proj, w_br_ret, w_br_gdn, w_out, tm=512):
    n = h.shape[0]
    tile = lambda blk: pl.BlockSpec((tm, D_MODEL), lambda i: (i, blk))
    weight = pl.BlockSpec((D_MODEL, D_MODEL), lambda i: (0, 0))
    return pl.pallas_call(
        _mix_out_kernel,
        grid=(n // tm,),
        in_specs=[tile(0), tile(0), tile(0), tile(_BLK_MGR), tile(_BLK_MGG), weight, weight, weight],
        out_specs=tile(0),
        out_shape=jax.ShapeDtypeStruct((n, D_MODEL), F32),
        compiler_params=_params("parallel"),
        name="mix_out",
    )(h, o_ret, o_gdn, proj, proj, w_br_ret, w_br_gdn, w_out)


def _ple_tail(h1, p_ref, gple_ref, wple_ref, wpg_ref, gfin_ref, out_ref, final):
    ple = _dot(p_ref[...].astype(BF16), wple_ref[...])
    gate = jax.nn.sigmoid(_dot(_rms(h1, gple_ref[...]).astype(BF16), wpg_ref[...]))
    h2 = h1 + gate * ple
    out_ref[...] = _rms(h2, gfin_ref[...]) if final else h2


def _ffn_kernel(h_ref, g_ref, wg_ref, wu_ref, wd_ref, p_ref, gple_ref, wple_ref, wpg_ref, gfin_ref,
                out_ref, *, final, tf):
    h = h_ref[...]
    u = _rms(h, g_ref[...]).astype(BF16)
    for j in range(D_FF // tf):
        cols = slice(j * tf, (j + 1) * tf)
        act = _silu(_dot(u, wg_ref[:, cols])) * _dot(u, wu_ref[:, cols])
        h = h + _dot(act.astype(BF16), wd_ref[cols, :])
    _ple_tail(h, p_ref, gple_ref, wple_ref, wpg_ref, gfin_ref, out_ref, final)


def _ffn(h, g, wg, wu, wd, p, g_ple, w_ple, w_ple_gate, g_final, final, tm=512, tf=D_FF // 2):
    n = h.shape[0]
    return pl.pallas_call(
        functools.partial(_ffn_kernel, final=final, tf=tf),
        grid=(n // tm,),
        in_specs=[
            pl.BlockSpec((tm, D_MODEL), lambda i: (i, 0)),
            _resident((1, D_MODEL)),
            _resident((D_MODEL, D_FF)),
            _resident((D_MODEL, D_FF)),
            _resident((D_FF, D_MODEL)),
            pl.BlockSpec((tm, PLE_DIM), lambda i: (i, 0)),
            _resident((1, D_MODEL)),
            _resident((PLE_DIM, D_MODEL)),
            _resident((D_MODEL, D_MODEL)),
            _resident((1, D_MODEL)),
        ],
        out_specs=pl.BlockSpec((tm, D_MODEL), lambda i: (i, 0)),
        out_shape=jax.ShapeDtypeStruct((n, D_MODEL), F32),
        compiler_params=_params("parallel"),
        name="ffn_ple",
    )(h, g, wg, wu, wd, p, g_ple, w_ple, w_ple_gate, g_final)


def _router_kernel(h_ref, g_ref, wr_ref, route_ref):
    u = _rms(h_ref[...], g_ref[...])
    w = wr_ref[...]
    u_hi, w_hi = u.astype(BF16), w.astype(BF16)
    u_lo = (u - u_hi.astype(F32)).astype(BF16)
    w_lo = (w - w_hi.astype(F32)).astype(BF16)
    logits = _dot(u_hi, w_hi) + (_dot(u_lo, w_hi) + _dot(u_hi, w_lo))
    lane = lax.broadcasted_iota(jnp.int32, logits.shape, 1)
    logits = jnp.where(lane < N_EXPERTS, logits, -jnp.inf)
    m1 = jnp.max(logits, axis=-1, keepdims=True)
    i1 = jnp.min(jnp.where(logits == m1, lane, LANES), axis=-1, keepdims=True)
    rest = jnp.where(lane == i1, -jnp.inf, logits)
    m2 = jnp.max(rest, axis=-1, keepdims=True)
    i2 = jnp.min(jnp.where(rest == m2, lane, LANES), axis=-1, keepdims=True)
    e2 = jnp.exp(m2 - m1)
    w1 = 1.0 / (1.0 + e2)
    route_ref[...] = (jnp.where(lane == 0, i1.astype(F32), 0.0) + jnp.where(lane == 1, i2.astype(F32), 0.0)
                      + jnp.where(lane == 2, w1, 0.0) + jnp.where(lane == 3, e2 * w1, 0.0))


def _router(h, g, w_router, tm=1024):
    n = h.shape[0]
    return pl.pallas_call(
        _router_kernel,
        grid=(n // tm,),
        in_specs=[
            pl.BlockSpec((tm, D_MODEL), lambda i: (i, 0)),
            pl.BlockSpec((1, D_MODEL), lambda i: (0, 0)),
            pl.BlockSpec((D_MODEL, LANES), lambda i: (0, 0)),
        ],
        out_specs=pl.BlockSpec((tm, LANES), lambda i: (i, 0)),
        out_shape=jax.ShapeDtypeStruct((n, LANES), F32),
        compiler_params=_params("parallel"),
        name="router",
    )(h, g, w_router)


def _routing_tables(route, n):
    n_pairs = 2 * n
    fill_bit = 1 << 15
    assert n_pairs <= fill_bit and MOE_TILE <= fill_bit
    expert = route[:, 0:2].astype(jnp.int32).reshape(-1)
    ids = jnp.arange(N_EXPERTS, dtype=jnp.int32)
    counts = jnp.sum((expert[:, None] == ids[None, :]).astype(jnp.int32), axis=0)
    padded = ((counts + MOE_TILE - 1) // MOE_TILE) * MOE_TILE
    keys = (expert << 16) | jnp.arange(n_pairs, dtype=jnp.int32)
    fill_idx = jnp.arange(MOE_TILE, dtype=jnp.int32)[None, :]
    fill_keys = jnp.where(fill_idx < (padded - counts)[:, None], (ids[:, None] << 16) | fill_bit | fill_idx,
                          jnp.iinfo(jnp.int32).max)
    keys = jnp.sort(jnp.concatenate([keys, fill_keys.reshape(-1)]))
    lookahead = jnp.full(((MOE_STAGES - 1) * MOE_TILE,), jnp.iinfo(jnp.int32).max, jnp.int32)
    keys = jnp.concatenate([keys, lookahead])
    low = keys & (2 * fill_bit - 1)
    valid = low < fill_bit
    row_expert = jnp.minimum(keys >> 16, N_EXPERTS - 1)
    spare = n_pairs + row_expert * MOE_TILE + jnp.minimum(low - fill_bit, MOE_TILE - 1)
    src_row = jnp.where(valid, low >> 1, 0)
    dst_row = jnp.where(valid, (low & 1) * n + (low >> 1), spare)
    tile_expert = row_expert[::MOE_TILE]
    n_used = (jnp.sum(padded) // MOE_TILE).reshape(1)
    return src_row, dst_row, tile_expert, n_used


def _rows_start(count, src_of, dst_of, sem, unrolled):
    def one(r):
        pltpu.make_async_copy(src_of(r), dst_of(r), sem).start()
    if unrolled:
        for r in range(count):
            one(r)
    else:
        def body(r, carry):
            one(r)
            return carry
        lax.fori_loop(0, count, body, 0, unroll=8)


def _moe_gemm_kernel(src_row_ref, dst_row_ref, tile_expert_ref, n_used_ref, h_hbm, g_ref, wg_ref, wu_ref,
                     wd_ref, y_hbm, xs_ref, ys_ref, gsem, ssem):
    del tile_expert_ref
    i = pl.program_id(0)
    last = n_used_ref[0] - 1
    rows = pl.ds(0, MOE_TILE)
    slot = lambda tile: tile % MOE_STAGES

    def gather_start(tile, unrolled):
        _rows_start(MOE_TILE, lambda r: h_hbm.at[pl.ds(src_row_ref[tile * MOE_TILE + r], 1)],
                    lambda r: xs_ref.at[slot(tile), pl.ds(r, 1)], gsem.at[slot(tile)], unrolled)

    def gather_wait(tile):
        pltpu.make_async_copy(h_hbm.at[rows], xs_ref.at[slot(tile)], gsem.at[slot(tile)]).wait()

    def scatter_start(tile, unrolled):
        _rows_start(MOE_TILE, lambda r: ys_ref.at[slot(tile), pl.ds(r, 1)],
                    lambda r: y_hbm.at[pl.ds(dst_row_ref[tile * MOE_TILE + r], 1)], ssem.at[slot(tile)], unrolled)

    def scatter_wait(tile):
        pltpu.make_async_copy(ys_ref.at[slot(tile)], y_hbm.at[rows], ssem.at[slot(tile)]).wait()

    def compute(tile):
        u = _rms(xs_ref[slot(tile)], g_ref[...]).astype(BF16)
        act = _silu(_dot(u, wg_ref[0])) * _dot(u, wu_ref[0])
        ys_ref[slot(tile)] = _dot(act.astype(BF16), wd_ref[0])

    @pl.when(i == 0)
    def _():
        for tile in range(MOE_STAGES):
            gather_start(tile, False)
        ys_ref[1] = jnp.zeros((MOE_TILE, D_MODEL), F32)
        n_pairs = y_hbm.shape[0] - N_EXPERTS * MOE_TILE
        for e in range(N_EXPERTS):
            block = pltpu.make_async_copy(ys_ref.at[1], y_hbm.at[pl.ds(n_pairs + e * MOE_TILE, MOE_TILE)],
                                          ssem.at[1])
            block.start()
            block.wait()
        gather_wait(0)
        compute(0)

    @pl.when(jnp.logical_and(i >= 1, i <= last))
    def _():
        gather_wait(i)
        gather_start(i + MOE_STAGES - 1, True)
        scatter_start(i - 1, True)
        compute(i)

    @pl.when(jnp.logical_and(i >= 2, i <= last))
    def _():
        scatter_wait(i - 2)

    @pl.when(i == last)
    def _():
        for ahead in range(1, MOE_STAGES):
            gather_wait(i + ahead)

        @pl.when(i >= 1)
        def _():
            scatter_wait(i - 1)

        scatter_start(i, False)
        scatter_wait(i)


def _moe_gemm(h, g, src_row, dst_row, tile_expert, n_used, wg, wu, wd):
    n = h.shape[0]
    n_tiles = src_row.shape[0] // MOE_TILE - (MOE_STAGES - 1)
    expert_block = lambda i, sr, dr, te, nu: (te[jnp.minimum(i, nu[0] - 1)], 0, 0)
    return pl.pallas_call(
        _moe_gemm_kernel,
        grid_spec=pltpu.PrefetchScalarGridSpec(
            num_scalar_prefetch=4,
            grid=(n_tiles,),
            in_specs=[
                pl.BlockSpec(memory_space=pl.ANY),
                pl.BlockSpec((1, D_MODEL), lambda i, sr, dr, te, nu: (0, 0)),
                pl.BlockSpec((1, D_MODEL, D_FF_EXPERT), expert_block),
                pl.BlockSpec((1, D_MODEL, D_FF_EXPERT), expert_block),
                pl.BlockSpec((1, D_FF_EXPERT, D_MODEL), expert_block),
            ],
            out_specs=pl.BlockSpec(memory_space=pl.ANY),
            scratch_shapes=[
                pltpu.VMEM((MOE_STAGES, MOE_TILE, D_MODEL), F32),
                pltpu.VMEM((MOE_STAGES, MOE_TILE, D_MODEL), F32),
                pltpu.SemaphoreType.DMA((MOE_STAGES,)),
                pltpu.SemaphoreType.DMA((MOE_STAGES,)),
            ],
        ),
        out_shape=jax.ShapeDtypeStruct((2 * n + N_EXPERTS * MOE_TILE, D_MODEL), F32),
        compiler_params=_params("arbitrary"),
        name="moe_gemm",
    )(src_row, dst_row, tile_expert, n_used, h, g, wg, wu, wd)


def _moe_tail_kernel(h_ref, route_ref, y0_ref, y1_ref, p_ref, gple_ref, wple_ref, wpg_ref, gfin_ref, out_ref,
                     *, final):
    route = route_ref[...]
    h1 = h_ref[...] + route[:, 2:3] * y0_ref[...] + route[:, 3:4] * y1_ref[...]
    _ple_tail(h1, p_ref, gple_ref, wple_ref, wpg_ref, gfin_ref, out_ref, final)


def _moe_tail(h, route, y, p, g_ple, w_ple, w_ple_gate, g_final, final, tm=512):
    n = h.shape[0]
    return pl.pallas_call(
        functools.partial(_moe_tail_kernel, final=final),
        grid=(n // tm,),
        in_specs=[
            pl.BlockSpec((tm, D_MODEL), lambda i: (i, 0)),
            pl.BlockSpec((tm, LANES), lambda i: (i, 0)),
            pl.BlockSpec((tm, D_MODEL), lambda i: (i, 0)),
            pl.BlockSpec((tm, D_MODEL), lambda i: (n // tm + i, 0)),
            pl.BlockSpec((tm, PLE_DIM), lambda i: (i, 0)),
            _resident((1, D_MODEL)),
            _resident((PLE_DIM, D_MODEL)),
            _resident((D_MODEL, D_MODEL)),
            _resident((1, D_MODEL)),
        ],
        out_specs=pl.BlockSpec((tm, D_MODEL), lambda i: (i, 0)),
        out_shape=jax.ShapeDtypeStruct((n, D_MODEL), F32),
        compiler_params=_params("parallel"),
        name="moe_tail",
    )(h, route, y, y, p, g_ple, w_ple, w_ple_gate, g_final)


def _moe(h, g, w_router, wg, wu, wd, p, g_ple, w_ple, w_ple_gate, g_final, final):
    n = h.shape[0]
    route = _router(h, g, w_router)
    src_row, dst_row, tile_expert, n_used = _routing_tables(route, n)
    y = _moe_gemm(h, g, src_row, dst_row, tile_expert, n_used, wg, wu, wd)
    return _moe_tail(h, route, y, p, g_ple, w_ple, w_ple_gate, g_final, final)


def _rotary_tables(seq):
    half = RET_DK // 2
    pos = jnp.arange(seq, dtype=F32)
    inv_freq = ROPE_THETA ** (-jnp.arange(half, dtype=F32) / half)
    ang = pos[:, None] * inv_freq[None, :]
    cos, sin = jnp.cos(ang), jnp.sin(ang)
    return jnp.concatenate([cos, cos], axis=-1), jnp.concatenate([-sin, sin], axis=-1)


def kernel(x, p, w_in, conv_w, a_log, dt_bias, ret_gn, gdn_gn, w_br_ret, w_br_gdn, w_out, norm_mix,
           norm_ffn, norm_ple, w_ple, w_ple_gate, ffn_w_gate, ffn_w_up, ffn_w_down, router,
           exp_w_gate, exp_w_up, exp_w_down, norm_final):
    batch, seq, _ = x.shape
    depth = w_in.shape[0]
    n = batch * seq
    cos_t, sin_t = _rotary_tables(seq)
    g_final = norm_final.reshape(1, -1)
    h = x.reshape(n, D_MODEL)
    for i in range(depth):
        w_gab = jnp.pad(w_in[i, :, OFF_GA:OFF_MG], ((0, 0), (0, LANES - 2 * GDN_HEADS))).astype(BF16)
        proj, gab = _in_proj(h, norm_mix[i].reshape(1, -1), w_in[i, :, :OFF_GA].astype(BF16),
                             w_in[i, :, OFF_MG:].astype(BF16), w_gab)
        o_ret = _retention(proj, cos_t, sin_t, ret_gn[i].reshape(1, -1), batch, seq)
        o_gdn = _gdn(proj, gab, conv_w[i], a_log[i], dt_bias[i], gdn_gn[i], batch, seq)
        h = _mix_out(h, o_ret, o_gdn, proj, w_br_ret[i].astype(BF16), w_br_gdn[i].astype(BF16),
                     w_out[i].astype(BF16))
        final = i == depth - 1
        tail = (p[i].reshape(n, PLE_DIM), norm_ple[i].reshape(1, -1), w_ple[i].astype(BF16),
                w_ple_gate[i].astype(BF16), g_final, final)
        j = i // 2
        g_ffn = norm_ffn[i].reshape(1, -1)
        if i % 2 == 0:
            h = _ffn(h, g_ffn, ffn_w_gate[j].astype(BF16), ffn_w_up[j].astype(BF16),
                     ffn_w_down[j].astype(BF16), *tail)
        else:
            w_router = jnp.pad(router[j], ((0, 0), (0, LANES - N_EXPERTS)))
            h = _moe(h, g_ffn, w_router, exp_w_gate[j].astype(BF16), exp_w_up[j].astype(BF16),
                     exp_w_down[j].astype(BF16), *tail)
    return h.reshape(batch, seq, D_MODEL)
```

```python
import functools

import numpy as np
import jax
import jax.numpy as jnp
from jax import lax
from jax.experimental import pallas as pl
from jax.experimental.pallas import tpu as pltpu

F32 = jnp.float32
BF16 = jnp.bfloat16

D_MODEL = 1024
PLE_DIM = 256
EPS = 1e-6
ROPE_THETA = 10000.0
RET_HEADS = 4
RET_DK = 128
RET_DV = 256
RET_QK = RET_HEADS * RET_DK
RET_V = RET_HEADS * RET_DV
GDN_HEADS = 8
GDN_DK = 128
GDN_DV = 128
GDN_QK = GDN_HEADS * GDN_DK
GDN_V = GDN_HEADS * GDN_DV
GDN_CONV_DIM = 2 * GDN_QK + GDN_V
CONV_WIDTH = 4
D_FF = 2816
N_EXPERTS = 8
D_FF_EXPERT = 1408
OFF_GA = 2 * RET_QK + 2 * RET_V + GDN_CONV_DIM + GDN_V
OFF_MG = OFF_GA + 2 * GDN_HEADS
N_MAIN = OFF_GA + 2 * D_MODEL
LANES = 128
GDN_CHUNK = 64
RET_CHUNK = 256
MOE_TILE = 512
MOE_STAGES = 3
TOKEN_TILE = (D_MODEL // LANES, LANES)
BF16_TILE = 16
VMEM_LIMIT = 56 * 1024 * 1024

_BLK_RV, _BLK_RG, _BLK_GQ, _BLK_GK, _BLK_GV, _BLK_GZ, _BLK_MGR, _BLK_MGG = 1, 2, 3, 4, 5, 6, 7, 8


def _silu(x):
    return x * jax.nn.sigmoid(x)


def _rms(x, g):
    return x * lax.rsqrt(jnp.mean(x * x, axis=-1, keepdims=True) + EPS) * g


def _dot(a, b):
    return jnp.dot(a, b, preferred_element_type=F32)


def _dot_nt(a, b):
    return lax.dot_general(a, b, (((1,), (1,)), ((), ())), preferred_element_type=F32)


def _params(*sem):
    return pltpu.CompilerParams(dimension_semantics=sem, vmem_limit_bytes=VMEM_LIMIT)


def _in_proj_kernel(x_ref, g_ref, wa_ref, wb_ref, wg_ref, om_ref, og_ref, *, tn):
    u = _rms(x_ref[...], g_ref[...]).astype(BF16)
    for j in range(N_MAIN // tn):
        w_ref, k = (wa_ref, j) if j < OFF_GA // tn else (wb_ref, j - OFF_GA // tn)
        om_ref[:, j * tn:(j + 1) * tn] = _dot(u, w_ref[:, k * tn:(k + 1) * tn]).astype(BF16)
    og_ref[...] = _dot(u, wg_ref[...])


def _resident(shape):
    return pl.BlockSpec(shape, lambda *_: (0,) * len(shape), pipeline_mode=pl.Buffered(1))


def _in_proj(h, g, w_a, w_b, w_gab, tm=512, tn=1024):
    n = h.shape[0]
    return pl.pallas_call(
        functools.partial(_in_proj_kernel, tn=tn),
        grid=(n // tm,),
        in_specs=[
            pl.BlockSpec((tm, D_MODEL), lambda i: (i, 0)),
            _resident((1, D_MODEL)),
            _resident((D_MODEL, OFF_GA)),
            _resident((D_MODEL, N_MAIN - OFF_GA)),
            _resident((D_MODEL, LANES)),
        ],
        out_specs=[
            pl.BlockSpec((tm, N_MAIN), lambda i: (i, 0)),
            pl.BlockSpec((tm, LANES), lambda i: (i, 0)),
        ],
        out_shape=[
            jax.ShapeDtypeStruct((n, N_MAIN), BF16),
            jax.ShapeDtypeStruct((n, LANES), F32),
        ],
        compiler_params=_params("parallel"),
        name="in_proj",
    )(h, g, w_a, w_b, w_gab)


def _ret_kernel(q_ref, k_ref, v_ref, rg_ref, cos_ref, sin_ref, dmask_ref, qdec_ref, kdec_ref,
                gn_ref, o_ref, state_ref, *, chunk_decay):
    @pl.when(pl.program_id(1) == 0)
    def _():
        state_ref[...] = jnp.zeros_like(state_ref)

    cos = cos_ref[...]
    sin = sin_ref[...]
    for hd in range(RET_HEADS):
        qk_sl = slice(hd * RET_DK, (hd + 1) * RET_DK)
        v_sl = slice(hd * RET_DV, (hd + 1) * RET_DV)
        q = q_ref[:, qk_sl].astype(F32)
        k = k_ref[:, qk_sl].astype(F32)
        q = q * cos + pltpu.roll(q, RET_DK // 2, 1) * sin
        k = (k * cos + pltpu.roll(k, RET_DK // 2, 1) * sin) * (RET_DK ** -0.5)
        v = v_ref[:, v_sl]
        scores = _dot_nt(q.astype(BF16), k.astype(BF16)) * dmask_ref[hd]
        state = state_ref[hd]
        o = _dot(scores.astype(BF16), v) + _dot((q * qdec_ref[:, qk_sl]).astype(BF16), state.astype(BF16))
        k_dec_t = (k * kdec_ref[:, qk_sl]).T.astype(BF16)
        state_ref[hd] = state * chunk_decay[hd] + _dot(k_dec_t, v)
        mu = jnp.mean(o, axis=-1, keepdims=True)
        oc = o - mu
        var = jnp.mean(oc * oc, axis=-1, keepdims=True)
        o = oc * lax.rsqrt(var + EPS) * gn_ref[:, v_sl]
        o_ref[:, v_sl] = (o * _silu(rg_ref[:, v_sl].astype(F32))).astype(BF16)


def _retention(proj, cos_t, sin_t, ret_gn, batch, seq):
    c = RET_CHUNK
    nc = seq // c
    log_gamma = np.log1p(-np.exp2(-5.0 - np.arange(RET_HEADS, dtype=np.float64)))
    chunk_decay = tuple(float(np.exp(c * lg)) for lg in log_gamma)
    lg = jnp.asarray(log_gamma, F32)
    idx = jnp.arange(c, dtype=F32)
    rel = idx[:, None] - idx[None, :]
    causal = rel >= 0
    dmask = jnp.where(causal[None], jnp.exp(jnp.where(causal, rel, 0.0)[None] * lg[:, None, None]), 0.0)
    qdec = jnp.repeat(jnp.exp((idx + 1.0)[:, None] * lg[None, :]), RET_DK, axis=1)
    kdec = jnp.repeat(jnp.exp((c - 1.0 - idx)[:, None] * lg[None, :]), RET_DK, axis=1)
    row = lambda b, t: b * nc + t
    return pl.pallas_call(
        functools.partial(_ret_kernel, chunk_decay=chunk_decay),
        grid=(batch, nc),
        in_specs=[
            pl.BlockSpec((c, RET_QK), lambda b, t: (row(b, t), 0)),
            pl.BlockSpec((c, RET_QK), lambda b, t: (row(b, t), 1)),
            pl.BlockSpec((c, RET_V), lambda b, t: (row(b, t), _BLK_RV)),
            pl.BlockSpec((c, RET_V), lambda b, t: (row(b, t), _BLK_RG)),
            pl.BlockSpec((c, RET_DK), lambda b, t: (t, 0)),
            pl.BlockSpec((c, RET_DK), lambda b, t: (t, 0)),
            pl.BlockSpec((RET_HEADS, c, c), lambda b, t: (0, 0, 0)),
            pl.BlockSpec((c, RET_QK), lambda b, t: (0, 0)),
            pl.BlockSpec((c, RET_QK), lambda b, t: (0, 0)),
            pl.BlockSpec((1, RET_V), lambda b, t: (0, 0)),
        ],
        out_specs=pl.BlockSpec((c, RET_V), lambda b, t: (row(b, t), 0)),
        out_shape=jax.ShapeDtypeStruct((batch * seq, RET_V), BF16),
        scratch_shapes=[pltpu.VMEM((RET_HEADS, RET_DK, RET_DV), F32)],
        compiler_params=_params("parallel", "arbitrary"),
        name="retention",
    )(proj, proj, proj, proj, cos_t, sin_t, dmask, qdec, kdec, ret_gn)


def _gdn_prepare_pieces(slot, gq_ref, gk_ref, gv_ref, gab_ref, cw_ref, shift_ref, alog_ref, dtb_ref, buf_ref,
                        qkv_ref, gate_ref):
    c = GDN_CHUNK

    def column_block(j):
        width = 2 * LANES
        src_ref = (gq_ref, gk_ref, gv_ref)[j * width // GDN_QK]
        src_sl = slice(j * width % GDN_QK, j * width % GDN_QK + width)
        sl = slice(j * width, (j + 1) * width)
        buf_ref[BF16_TILE:BF16_TILE + c, sl] = src_ref[:, src_sl]
        shifted = _dot(shift_ref[...], buf_ref[:, sl])
        buf_ref[0:BF16_TILE, sl] = buf_ref[c:c + BF16_TILE, sl]
        y = shifted[0:c] * cw_ref[CONV_WIDTH - 1:CONV_WIDTH, sl]
        for s in range(1, CONV_WIDTH):
            y = y + shifted[s * c:(s + 1) * c] * cw_ref[CONV_WIDTH - 1 - s:CONV_WIDTH - s, sl]
        y = _silu(y)
        for half in range(2):
            hsl = slice(half * LANES, (half + 1) * LANES)
            yh = y[:, hsl]
            if sl.start < 2 * GDN_QK:
                scale = GDN_DK ** -0.5 if sl.start < GDN_QK else 1.0
                yh = yh * (lax.rsqrt(jnp.sum(yh * yh, axis=-1, keepdims=True) + EPS) * scale)
            qkv_ref[slot, :, sl.start + half * LANES:sl.start + (half + 1) * LANES] = yh

    def gates():
        gab = gab_ref[...]
        z = gab + dtb_ref[...]
        softplus = jnp.maximum(z, 0.0) + jnp.log(1.0 + jnp.exp(-jnp.abs(z)))
        gc = -jnp.exp(alog_ref[...]) * softplus
        row = lax.broadcasted_iota(jnp.int32, (c, LANES), 0)
        for s in (1, 2, 4, 8, 16, 32):
            gc = gc + jnp.where(row >= s, pltpu.roll(gc, s, 0), 0.0)
        gate_ref[slot, 0] = gc
        gate_ref[slot, 1] = jax.nn.sigmoid(gab)

    return [functools.partial(column_block, j) for j in range(GDN_CONV_DIM // (2 * LANES))] + [gates]


def _gdn_delta_rule_stages(slot, gz_ref, gn_ref, o_ref, state_ref, qkv_ref, gate_ref):
    c = GDN_CHUNK
    gc = gate_ref[slot, 0]
    beta = gate_ref[slot, 1]
    gc_t = gc.T
    ii = lax.broadcasted_iota(jnp.int32, (c, c), 0)
    jj = lax.broadcasted_iota(jnp.int32, (c, c), 1)
    incl = ii >= jj
    strict = ii > jj
    gn = gn_ref[...]
    hs = range(GDN_HEADS)
    sls = [slice(hd * GDN_DK, (hd + 1) * GDN_DK) for hd in hs]
    q_l = [qkv_ref[slot, :, sls[hd]] for hd in hs]
    k_l = [qkv_ref[slot, :, GDN_QK + hd * GDN_DK:GDN_QK + (hd + 1) * GDN_DK] for hd in hs]
    v_l = [qkv_ref[slot, :, 2 * GDN_QK + hd * GDN_DV:2 * GDN_QK + (hd + 1) * GDN_DV] for hd in hs]
    gcol_l = [gc[:, hd:hd + 1] for hd in hs]
    glast_l = [gc[c - 1:c, hd:hd + 1] for hd in hs]
    bcol_l = [beta[:, GDN_HEADS + hd:GDN_HEADS + hd + 1] for hd in hs]
    m_l, x_l, qk_l, ecol_l = [], [], [], []
    for hd in hs:
        decay = jnp.where(incl, jnp.exp(jnp.where(incl, gcol_l[hd] - gc_t[hd:hd + 1, :], 0.0)), 0.0)
        kb = k_l[hd] * bcol_l[hd]
        both = _dot_nt(jnp.concatenate([kb, q_l[hd]], axis=0).astype(BF16), k_l[hd].astype(BF16))
        m_l.append(-(both[0:c] * jnp.where(strict, decay, 0.0)))
        qk_l.append(both[c:2 * c] * decay)
        e_col = jnp.exp(gcol_l[hd])
        ecol_l.append(e_col)
        x_l.append(jnp.concatenate([v_l[hd] * bcol_l[hd], kb * e_col], axis=1))
    yield
    eye = (ii == jj).astype(F32)
    t_l = [eye + m for m in m_l]
    mb_l = [m.astype(BF16) for m in m_l]
    m_l = [_dot(mb_l[hd], mb_l[hd]) for hd in hs]
    for it in range(1, 6):
        mb_l = [m.astype(BF16) for m in m_l]
        if it < 5:
            prod = [_dot(jnp.concatenate([m_l[hd], t_l[hd]], axis=0).astype(BF16), mb_l[hd]) for hd in hs]
            m_next = [pr[0:c] for pr in prod]
            t_l = [t_l[hd] + prod[hd][c:2 * c] for hd in hs]
            m_l = m_next
        else:
            t_l = [t_l[hd] + _dot(t_l[hd].astype(BF16), mb_l[hd]) for hd in hs]
        yield
    x_l = [_dot(t_l[hd].astype(BF16), x_l[hd].astype(BF16)) for hd in hs]
    yield
    st_l = [state_ref[hd] for hd in hs]
    ws_l = [_dot(jnp.concatenate([x_l[hd][:, GDN_DV:], q_l[hd] * ecol_l[hd]], axis=0).astype(BF16),
                 st_l[hd].astype(BF16)) for hd in hs]
    vn_l = [(x_l[hd][:, 0:GDN_DV] - ws_l[hd][0:c]).astype(BF16) for hd in hs]
    yield
    o_l = [ws_l[hd][c:2 * c] + _dot(qk_l[hd].astype(BF16), vn_l[hd]) for hd in hs]
    for hd in hs:
        k_dec_t = (k_l[hd] * jnp.exp(glast_l[hd] - gcol_l[hd])).T.astype(BF16)
        state_ref[hd] = st_l[hd] * jnp.exp(glast_l[hd]) + _dot(k_dec_t, vn_l[hd])
    yield
    for hd in hs:
        o = o_l[hd]
        o = o * lax.rsqrt(jnp.mean(o * o, axis=-1, keepdims=True) + EPS) * gn
        o_ref[:, sls[hd]] = (o * _silu(gz_ref[:, sls[hd]].astype(F32))).astype(BF16)


def _gdn_kernel(gq_ref, gk_ref, gv_ref, gz_ref, gab_ref, cw_ref, shift_ref, alog_ref, dtb_ref, gn_ref,
                o_ref, state_ref, buf_ref, qkv_ref, gate_ref):
    t = pl.program_id(1)

    @pl.when(t == 0)
    def _():
        state_ref[...] = jnp.zeros_like(state_ref)
        buf_ref[0:BF16_TILE, :] = jnp.zeros((BF16_TILE, GDN_CONV_DIM), BF16)
        qkv_ref[...] = jnp.zeros_like(qkv_ref)
        gate_ref[...] = jnp.zeros_like(gate_ref)

    slot = t % 2
    pieces = _gdn_prepare_pieces(slot, gq_ref, gk_ref, gv_ref, gab_ref, cw_ref, shift_ref, alog_ref, dtb_ref,
                                 buf_ref, qkv_ref, gate_ref)
    per_stage = 2
    for _ in _gdn_delta_rule_stages(1 - slot, gz_ref, gn_ref, o_ref, state_ref, qkv_ref, gate_ref):
        for piece in pieces[:per_stage]:
            piece()
        pieces = pieces[per_stage:]
    for piece in pieces:
        piece()


def _gdn(proj, gab, conv_w, a_log, dt_bias, gdn_gn, batch, seq):
    c = GDN_CHUNK
    nc = seq // c
    row = lambda b, t: b * nc + jnp.minimum(t, nc - 1)
    prev = lambda b, t: b * nc + jnp.maximum(t - 1, 0)
    pad = lambda a: jnp.pad(a.reshape(1, -1), ((0, 0), (0, LANES - a.shape[-1])))
    shift = np.zeros((CONV_WIDTH * c, BF16_TILE + c), np.float32)
    for s in range(CONV_WIDTH):
        shift[s * c + np.arange(c), BF16_TILE + np.arange(c) - s] = 1.0
    return pl.pallas_call(
        _gdn_kernel,
        grid=(batch, nc + 1),
        in_specs=[
            pl.BlockSpec((c, GDN_QK), lambda b, t: (row(b, t), _BLK_GQ)),
            pl.BlockSpec((c, GDN_QK), lambda b, t: (row(b, t), _BLK_GK)),
            pl.BlockSpec((c, GDN_V), lambda b, t: (row(b, t), _BLK_GV)),
            pl.BlockSpec((c, GDN_V), lambda b, t: (prev(b, t), _BLK_GZ)),
            pl.BlockSpec((c, LANES), lambda b, t: (row(b, t), 0)),
            pl.BlockSpec((CONV_WIDTH, GDN_CONV_DIM), lambda b, t: (0, 0)),
            pl.BlockSpec((CONV_WIDTH * c, BF16_TILE + c), lambda b, t: (0, 0)),
            pl.BlockSpec((1, LANES), lambda b, t: (0, 0)),
            pl.BlockSpec((1, LANES), lambda b, t: (0, 0)),
            pl.BlockSpec((1, GDN_DV), lambda b, t: (0, 0)),
        ],
        out_specs=pl.BlockSpec((c, GDN_V), lambda b, t: (prev(b, t), 0)),
        out_shape=jax.ShapeDtypeStruct((batch * seq, GDN_V), BF16),
        scratch_shapes=[
            pltpu.VMEM((GDN_HEADS, GDN_DK, GDN_DV), F32),
            pltpu.VMEM((BF16_TILE + c, GDN_CONV_DIM), BF16),
            pltpu.VMEM((2, c, GDN_CONV_DIM), F32),
            pltpu.VMEM((2, 2, c, LANES), F32),
        ],
        compiler_params=_params("parallel", "arbitrary"),
        name="gdn",
    )(proj, proj, proj, proj, gab, conv_w, jnp.asarray(shift, BF16), pad(a_log), pad(dt_bias), gdn_gn.reshape(1, -1))


def _to_token_tiles(x, ref, lead=()):
    for s in range(D_MODEL // LANES):
        ref[lead + (slice(None), s, slice(None))] = x[:, s * LANES:(s + 1) * LANES]


def _from_token_tiles(ref, lead=()):
    return jnp.concatenate([ref[lead + (slice(None), s, slice(None))] for s in range(D_MODEL // LANES)], axis=-1)


def _mix_out_kernel(h_ref, oret_ref, ogdn_ref, mgr_ref, mgg_ref, wbr_ref, wbg_ref, wo_ref, out_ref, *tok_ref):
    y_ret = _dot(oret_ref[...], wbr_ref[...])
    y_gdn = _dot(ogdn_ref[...], wbg_ref[...])
    merged = (jax.nn.sigmoid(mgr_ref[...].astype(F32)) * y_ret
              + jax.nn.sigmoid(mgg_ref[...].astype(F32)) * y_gdn)
    out = h_ref[...] + _dot(merged.astype(BF16), wo_ref[...])
    out_ref[...] = out
    if tok_ref:
        _to_token_tiles(out, tok_ref[0])


def _mix_out(h, o_ret, o_gdn, proj, w_br_ret, w_br_gdn, w_out, token_tiles, tm=512):
    n = h.shape[0]
    tile = lambda blk: pl.BlockSpec((tm, D_MODEL), lambda i: (i, blk))
    weight = pl.BlockSpec((D_MODEL, D_MODEL), lambda i: (0, 0))
    out_specs = [tile(0)]
    out_shape = [jax.ShapeDtypeStruct((n, D_MODEL), F32)]
    if token_tiles:
        out_specs.append(pl.BlockSpec((tm,) + TOKEN_TILE, lambda i: (i, 0, 0)))
        out_shape.append(jax.ShapeDtypeStruct((n,) + TOKEN_TILE, F32))
    return pl.pallas_call(
        _mix_out_kernel,
        grid=(n // tm,),
        in_specs=[tile(0), tile(0), tile(0), tile(_BLK_MGR), tile(_BLK_MGG), weight, weight, weight],
        out_specs=out_specs,
        out_shape=out_shape,
        compiler_params=_params("parallel"),
        name="mix_out",
    )(h, o_ret, o_gdn, proj, proj, w_br_ret, w_br_gdn, w_out)


def _ple_tail(h1, p_ref, gple_ref, wple_ref, wpg_ref, gfin_ref, out_ref, final):
    ple = _dot(p_ref[...].astype(BF16), wple_ref[...])
    gate = jax.nn.sigmoid(_dot(_rms(h1, gple_ref[...]).astype(BF16), wpg_ref[...]))
    h2 = h1 + gate * ple
    out_ref[...] = _rms(h2, gfin_ref[...]) if final else h2


def _ffn_kernel(h_ref, g_ref, wg_ref, wu_ref, wd_ref, p_ref, gple_ref, wple_ref, wpg_ref, gfin_ref,
                out_ref, *, final, tf):
    h = h_ref[...]
    u = _rms(h, g_ref[...]).astype(BF16)
    for j in range(D_FF // tf):
        cols = slice(j * tf, (j + 1) * tf)
        act = _silu(_dot(u, wg_ref[:, cols])) * _dot(u, wu_ref[:, cols])
        h = h + _dot(act.astype(BF16), wd_ref[cols, :])
    _ple_tail(h, p_ref, gple_ref, wple_ref, wpg_ref, gfin_ref, out_ref, final)


def _ffn(h, g, wg, wu, wd, p, g_ple, w_ple, w_ple_gate, g_final, final, tm=512, tf=D_FF // 2):
    n = h.shape[0]
    return pl.pallas_call(
        functools.partial(_ffn_kernel, final=final, tf=tf),
        grid=(n // tm,),
        in_specs=[
            pl.BlockSpec((tm, D_MODEL), lambda i: (i, 0)),
            _resident((1, D_MODEL)),
            _resident((D_MODEL, D_FF)),
            _resident((D_MODEL, D_FF)),
            _resident((D_FF, D_MODEL)),
            pl.BlockSpec((tm, PLE_DIM), lambda i: (i, 0)),
            _resident((1, D_MODEL)),
            _resident((PLE_DIM, D_MODEL)),
            _resident((D_MODEL, D_MODEL)),
            _resident((1, D_MODEL)),
        ],
        out_specs=pl.BlockSpec((tm, D_MODEL), lambda i: (i, 0)),
        out_shape=jax.ShapeDtypeStruct((n, D_MODEL), F32),
        compiler_params=_params("parallel"),
        name="ffn_ple",
    )(h, g, wg, wu, wd, p, g_ple, w_ple, w_ple_gate, g_final)


def _router_kernel(h_ref, g_ref, wr_ref, route_ref):
    u = _rms(h_ref[...], g_ref[...])
    w = wr_ref[...]
    u_hi, w_hi = u.astype(BF16), w.astype(BF16)
    u_lo = (u - u_hi.astype(F32)).astype(BF16)
    w_lo = (w - w_hi.astype(F32)).astype(BF16)
    logits = _dot(u_hi, w_hi) + (_dot(u_lo, w_hi) + _dot(u_hi, w_lo))
    lane = lax.broadcasted_iota(jnp.int32, logits.shape, 1)
    logits = jnp.where(lane < N_EXPERTS, logits, -jnp.inf)
    m1 = jnp.max(logits, axis=-1, keepdims=True)
    i1 = jnp.min(jnp.where(logits == m1, lane, LANES), axis=-1, keepdims=True)
    rest = jnp.where(lane == i1, -jnp.inf, logits)
    m2 = jnp.max(rest, axis=-1, keepdims=True)
    i2 = jnp.min(jnp.where(rest == m2, lane, LANES), axis=-1, keepdims=True)
    e2 = jnp.exp(m2 - m1)
    w1 = 1.0 / (1.0 + e2)
    route_ref[...] = (jnp.where(lane == 0, i1.astype(F32), 0.0) + jnp.where(lane == 1, i2.astype(F32), 0.0)
                      + jnp.where(lane == 2, w1, 0.0) + jnp.where(lane == 3, e2 * w1, 0.0))


def _router(h, g, w_router, tm=1024):
    n = h.shape[0]
    return pl.pallas_call(
        _router_kernel,
        grid=(n // tm,),
        in_specs=[
            pl.BlockSpec((tm, D_MODEL), lambda i: (i, 0)),
            pl.BlockSpec((1, D_MODEL), lambda i: (0, 0)),
            pl.BlockSpec((D_MODEL, LANES), lambda i: (0, 0)),
        ],
        out_specs=pl.BlockSpec((tm, LANES), lambda i: (i, 0)),
        out_shape=jax.ShapeDtypeStruct((n, LANES), F32),
        compiler_params=_params("parallel"),
        name="router",
    )(h, g, w_router)


def _routing_tables(route, n):
    n_pairs = 2 * n
    fill_bit = 1 << 15
    assert n_pairs <= fill_bit and MOE_TILE <= fill_bit
    expert = route[:, 0:2].astype(jnp.int32).reshape(-1)
    ids = jnp.arange(N_EXPERTS, dtype=jnp.int32)
    counts = jnp.sum((expert[:, None] == ids[None, :]).astype(jnp.int32), axis=0)
    padded = ((counts + MOE_TILE - 1) // MOE_TILE) * MOE_TILE
    keys = (expert << 16) | jnp.arange(n_pairs, dtype=jnp.int32)
    fill_idx = jnp.arange(MOE_TILE, dtype=jnp.int32)[None, :]
    fill_keys = jnp.where(fill_idx < (padded - counts)[:, None], (ids[:, None] << 16) | fill_bit | fill_idx,
                          jnp.iinfo(jnp.int32).max)
    keys = jnp.sort(jnp.concatenate([keys, fill_keys.reshape(-1)]))
    lookahead = jnp.full(((MOE_STAGES - 1) * MOE_TILE,), jnp.iinfo(jnp.int32).max, jnp.int32)
    keys = jnp.concatenate([keys, lookahead])
    low = keys & (2 * fill_bit - 1)
    valid = low < fill_bit
    row_expert = jnp.minimum(keys >> 16, N_EXPERTS - 1)
    spare = n_pairs + row_expert * MOE_TILE + jnp.minimum(low - fill_bit, MOE_TILE - 1)
    src_row = jnp.where(valid, low >> 1, 0)
    dst_row = jnp.where(valid, (low & 1) * n + (low >> 1), spare)
    tile_expert = row_expert[::MOE_TILE]
    n_used = (jnp.sum(padded) // MOE_TILE).reshape(1)
    return src_row, dst_row, tile_expert, n_used


def _rows_start(count, src_of, dst_of, sem, unrolled):
    def one(r):
        pltpu.make_async_copy(src_of(r), dst_of(r), sem).start()
    if unrolled:
        for r in range(count):
            one(r)
    else:
        def body(r, carry):
            one(r)
            return carry
        lax.fori_loop(0, count, body, 0, unroll=8)


def _moe_gemm_kernel(src_row_ref, dst_row_ref, tile_expert_ref, n_used_ref, h_hbm, g_ref, wg_ref, wu_ref,
                     wd_ref, y_hbm, xs_ref, ys_ref, gsem, ssem):
    del tile_expert_ref
    i = pl.program_id(0)
    last = n_used_ref[0] - 1
    rows = pl.ds(0, MOE_TILE)
    slot = lambda tile: tile % MOE_STAGES

    def gather_start(tile, unrolled):
        _rows_start(MOE_TILE, lambda r: h_hbm.at[pl.ds(src_row_ref[tile * MOE_TILE + r], 1)],
                    lambda r: xs_ref.at[slot(tile), pl.ds(r, 1)], gsem.at[slot(tile)], unrolled)

    def gather_wait(tile):
        pltpu.make_async_copy(h_hbm.at[rows], xs_ref.at[slot(tile)], gsem.at[slot(tile)]).wait()

    def scatter_start(tile, unrolled):
        _rows_start(MOE_TILE, lambda r: ys_ref.at[slot(tile), pl.ds(r, 1)],
                    lambda r: y_hbm.at[pl.ds(dst_row_ref[tile * MOE_TILE + r], 1)], ssem.at[slot(tile)], unrolled)

    def scatter_wait(tile):
        pltpu.make_async_copy(ys_ref.at[slot(tile)], y_hbm.at[rows], ssem.at[slot(tile)]).wait()

    def compute(tile):
        u = _rms(_from_token_tiles(xs_ref, (slot(tile),)), g_ref[...]).astype(BF16)
        act = _silu(_dot(u, wg_ref[0])) * _dot(u, wu_ref[0])
        _to_token_tiles(_dot(act.astype(BF16), wd_ref[0]), ys_ref, (slot(tile),))

    @pl.when(i == 0)
    def _():
        for tile in range(MOE_STAGES):
            gather_start(tile, False)
        ys_ref[1] = jnp.zeros((MOE_TILE,) + TOKEN_TILE, F32)
        n_pairs = y_hbm.shape[0] - N_EXPERTS * MOE_TILE
        for e in range(N_EXPERTS):
            block = pltpu.make_async_copy(ys_ref.at[1], y_hbm.at[pl.ds(n_pairs + e * MOE_TILE, MOE_TILE)],
                                          ssem.at[1])
            block.start()
            block.wait()
        gather_wait(0)
        compute(0)

    @pl.when(jnp.logical_and(i >= 1, i <= last))
    def _():
        gather_wait(i)
        gather_start(i + MOE_STAGES - 1, True)
        scatter_start(i - 1, True)
        compute(i)

    @pl.when(jnp.logical_and(i >= 2, i <= last))
    def _():
        scatter_wait(i - 2)

    @pl.when(i == last)
    def _():
        for ahead in range(1, MOE_STAGES):
            gather_wait(i + ahead)

        @pl.when(i >= 1)
        def _():
            scatter_wait(i - 1)

        scatter_start(i, False)
        scatter_wait(i)


def _moe_gemm(h_tok, g, src_row, dst_row, tile_expert, n_used, wg, wu, wd):
    n = h_tok.shape[0]
    n_tiles = src_row.shape[0] // MOE_TILE - (MOE_STAGES - 1)
    expert_block = lambda i, sr, dr, te, nu: (te[jnp.minimum(i, nu[0] - 1)], 0, 0)
    return pl.pallas_call(
        _moe_gemm_kernel,
        grid_spec=pltpu.PrefetchScalarGridSpec(
            num_scalar_prefetch=4,
            grid=(n_tiles,),
            in_specs=[
                pl.BlockSpec(memory_space=pl.ANY),
                pl.BlockSpec((1, D_MODEL), lambda i, sr, dr, te, nu: (0, 0)),
                pl.BlockSpec((1, D_MODEL, D_FF_EXPERT), expert_block),
                pl.BlockSpec((1, D_MODEL, D_FF_EXPERT), expert_block),
                pl.BlockSpec((1, D_FF_EXPERT, D_MODEL), expert_block),
            ],
            out_specs=pl.BlockSpec(memory_space=pl.ANY),
            scratch_shapes=[
                pltpu.VMEM((MOE_STAGES, MOE_TILE) + TOKEN_TILE, F32),
                pltpu.VMEM((MOE_STAGES, MOE_TILE) + TOKEN_TILE, F32),
                pltpu.SemaphoreType.DMA((MOE_STAGES,)),
                pltpu.SemaphoreType.DMA((MOE_STAGES,)),
            ],
        ),
        out_shape=jax.ShapeDtypeStruct((2 * n + N_EXPERTS * MOE_TILE,) + TOKEN_TILE, F32),
        compiler_params=_params("arbitrary"),
        name="moe_gemm",
    )(src_row, dst_row, tile_expert, n_used, h_tok, g, wg, wu, wd)


def _moe_tail_kernel(h_ref, route_ref, y0_ref, y1_ref, p_ref, gple_ref, wple_ref, wpg_ref, gfin_ref, out_ref,
                     *, final):
    route = route_ref[...]
    h1 = h_ref[...] + route[:, 2:3] * _from_token_tiles(y0_ref) + route[:, 3:4] * _from_token_tiles(y1_ref)
    _ple_tail(h1, p_ref, gple_ref, wple_ref, wpg_ref, gfin_ref, out_ref, final)


def _moe_tail(h, route, y, p, g_ple, w_ple, w_ple_gate, g_final, final, tm=512):
    n = h.shape[0]
    return pl.pallas_call(
        functools.partial(_moe_tail_kernel, final=final),
        grid=(n // tm,),
        in_specs=[
            pl.BlockSpec((tm, D_MODEL), lambda i: (i, 0)),
            pl.BlockSpec((tm, LANES), lambda i: (i, 0)),
            pl.BlockSpec((tm,) + TOKEN_TILE, lambda i: (i, 0, 0)),
            pl.BlockSpec((tm,) + TOKEN_TILE, lambda i: (n // tm + i, 0, 0)),
            pl.BlockSpec((tm, PLE_DIM), lambda i: (i, 0)),
            _resident((1, D_MODEL)),
            _resident((PLE_DIM, D_MODEL)),
            _resident((D_MODEL, D_MODEL)),
            _resident((1, D_MODEL)),
        ],
        out_specs=pl.BlockSpec((tm, D_MODEL), lambda i: (i, 0)),
        out_shape=jax.ShapeDtypeStruct((n, D_MODEL), F32),
        compiler_params=_params("parallel"),
        name="moe_tail",
    )(h, route, y, y, p, g_ple, w_ple, w_ple_gate, g_final)


def _moe(h, h_tok, g, w_router, wg, wu, wd, p, g_ple, w_ple, w_ple_gate, g_final, final):
    n = h.shape[0]
    route = _router(h, g, w_router)
    src_row, dst_row, tile_expert, n_used = _routing_tables(route, n)
    y = _moe_gemm(h_tok, g, src_row, dst_row, tile_expert, n_used, wg, wu, wd)
    return _moe_tail(h, route, y, p, g_ple, w_ple, w_ple_gate, g_final, final)


def _rotary_tables(seq):
    half = RET_DK // 2
    pos = jnp.arange(seq, dtype=F32)
    inv_freq = ROPE_THETA ** (-jnp.arange(half, dtype=F32) / half)
    ang = pos[:, None] * inv_freq[None, :]
    cos, sin = jnp.cos(ang), jnp.sin(ang)
    return jnp.concatenate([cos, cos], axis=-1), jnp.concatenate([-sin, sin], axis=-1)


def kernel(x, p, w_in, conv_w, a_log, dt_bias, ret_gn, gdn_gn, w_br_ret, w_br_gdn, w_out, norm_mix,
           norm_ffn, norm_ple, w_ple, w_ple_gate, ffn_w_gate, ffn_w_up, ffn_w_down, router,
           exp_w_gate, exp_w_up, exp_w_down, norm_final):
    batch, seq, _ = x.shape
    depth = w_in.shape[0]
    n = batch * seq
    cos_t, sin_t = _rotary_tables(seq)
    g_final = norm_final.reshape(1, -1)
    h = x.reshape(n, D_MODEL)
    for i in range(depth):
        w_gab = jnp.pad(w_in[i, :, OFF_GA:OFF_MG], ((0, 0), (0, LANES - 2 * GDN_HEADS))).astype(BF16)
        proj, gab = _in_proj(h, norm_mix[i].reshape(1, -1), w_in[i, :, :OFF_GA].astype(BF16),
                             w_in[i, :, OFF_MG:].astype(BF16), w_gab)
        o_ret = _retention(proj, cos_t, sin_t, ret_gn[i].reshape(1, -1), batch, seq)
        o_gdn = _gdn(proj, gab, conv_w[i], a_log[i], dt_bias[i], gdn_gn[i], batch, seq)
        routed = i % 2 == 1
        h, *h_tok = _mix_out(h, o_ret, o_gdn, proj, w_br_ret[i].astype(BF16), w_br_gdn[i].astype(BF16),
                             w_out[i].astype(BF16), token_tiles=routed)
        final = i == depth - 1
        tail = (p[i].reshape(n, PLE_DIM), norm_ple[i].reshape(1, -1), w_ple[i].astype(BF16),
                w_ple_gate[i].astype(BF16), g_final, final)
        j = i // 2
        g_ffn = norm_ffn[i].reshape(1, -1)
        if not routed:
            h = _ffn(h, g_ffn, ffn_w_gate[j].astype(BF16), ffn_w_up[j].astype(BF16),
                     ffn_w_down[j].astype(BF16), *tail)
        else:
            w_router = jnp.pad(router[j], ((0, 0), (0, LANES - N_EXPERTS)))
            h = _moe(h, h_tok[0], g_ffn, w_router, exp_w_gate[j].astype(BF16), exp_w_up[j].astype(BF16),
                     exp_w_down[j].astype(BF16), *tail)
    return h.reshape(batch, seq, D_MODEL)
```

```python
import functools

import numpy as np
import jax
import jax.numpy as jnp
from jax import lax
from jax.experimental import pallas as pl
from jax.experimental.pallas import tpu as pltpu

F32 = jnp.float32
BF16 = jnp.bfloat16

D_MODEL = 1024
PLE_DIM = 256
EPS = 1e-6
ROPE_THETA = 10000.0
RET_HEADS = 4
RET_DK = 128
RET_DV = 256
RET_QK = RET_HEADS * RET_DK
RET_V = RET_HEADS * RET_DV
GDN_HEADS = 8
GDN_DK = 128
GDN_DV = 128
GDN_QK = GDN_HEADS * GDN_DK
GDN_V = GDN_HEADS * GDN_DV
GDN_CONV_DIM = 2 * GDN_QK + GDN_V
CONV_WIDTH = 4
D_FF = 2816
N_EXPERTS = 8
D_FF_EXPERT = 1408
OFF_GA = 2 * RET_QK + 2 * RET_V + GDN_CONV_DIM + GDN_V
OFF_MG = OFF_GA + 2 * GDN_HEADS
N_MAIN = OFF_GA + 2 * D_MODEL
LANES = 128
GDN_CHUNK = 64
GDN_STEP_CHUNKS = 4
RET_CHUNK = 256
MOE_TILE = 512
MOE_STAGES = 3
BF16_TILE = 16
VMEM_LIMIT = 56 * 1024 * 1024

_BLK_RV, _BLK_RG, _BLK_GQ, _BLK_GK, _BLK_GV, _BLK_GZ, _BLK_MGR, _BLK_MGG = 1, 2, 3, 4, 5, 6, 7, 8


def _silu(x):
    return x * jax.nn.sigmoid(x)


def _rms(x, g):
    return x * lax.rsqrt(jnp.mean(x * x, axis=-1, keepdims=True) + EPS) * g


def _dot(a, b):
    return jnp.dot(a, b, preferred_element_type=F32)


def _dot_nt(a, b):
    return lax.dot_general(a, b, (((1,), (1,)), ((), ())), preferred_element_type=F32)


def _params(*sem):
    return pltpu.CompilerParams(dimension_semantics=sem, vmem_limit_bytes=VMEM_LIMIT)


def _in_proj_kernel(x_ref, g_ref, wa_ref, wb_ref, wg_ref, om_ref, og_ref, *, tn):
    u = _rms(x_ref[...], g_ref[...]).astype(BF16)
    for j in range(N_MAIN // tn):
        w_ref, k = (wa_ref, j) if j < OFF_GA // tn else (wb_ref, j - OFF_GA // tn)
        om_ref[:, j * tn:(j + 1) * tn] = _dot(u, w_ref[:, k * tn:(k + 1) * tn]).astype(BF16)
    og_ref[...] = _dot(u, wg_ref[...])


def _resident(shape):
    return pl.BlockSpec(shape, lambda *_: (0,) * len(shape), pipeline_mode=pl.Buffered(1))


def _in_proj(h, g, w_a, w_b, w_gab, tm=512, tn=1024):
    n = h.shape[0]
    return pl.pallas_call(
        functools.partial(_in_proj_kernel, tn=tn),
        grid=(n // tm,),
        in_specs=[
            pl.BlockSpec((tm, D_MODEL), lambda i: (i, 0)),
            _resident((1, D_MODEL)),
            _resident((D_MODEL, OFF_GA)),
            _resident((D_MODEL, N_MAIN - OFF_GA)),
            _resident((D_MODEL, LANES)),
        ],
        out_specs=[
            pl.BlockSpec((tm, N_MAIN), lambda i: (i, 0)),
            pl.BlockSpec((tm, LANES), lambda i: (i, 0)),
        ],
        out_shape=[
            jax.ShapeDtypeStruct((n, N_MAIN), BF16),
            jax.ShapeDtypeStruct((n, LANES), F32),
        ],
        compiler_params=_params("parallel"),
        name="in_proj",
    )(h, g, w_a, w_b, w_gab)


def _ret_kernel(q_ref, k_ref, v_ref, rg_ref, cos_ref, sin_ref, dmask_ref, qdec_ref, kdec_ref,
                gn_ref, o_ref, state_ref, *, chunk_decay):
    @pl.when(pl.program_id(1) == 0)
    def _():
        state_ref[...] = jnp.zeros_like(state_ref)

    cos = cos_ref[...]
    sin = sin_ref[...]
    for hd in range(RET_HEADS):
        qk_sl = slice(hd * RET_DK, (hd + 1) * RET_DK)
        v_sl = slice(hd * RET_DV, (hd + 1) * RET_DV)
        q = q_ref[:, qk_sl].astype(F32)
        k = k_ref[:, qk_sl].astype(F32)
        q = q * cos + pltpu.roll(q, RET_DK // 2, 1) * sin
        k = (k * cos + pltpu.roll(k, RET_DK // 2, 1) * sin) * (RET_DK ** -0.5)
        v = v_ref[:, v_sl]
        scores = _dot_nt(q.astype(BF16), k.astype(BF16)) * dmask_ref[hd]
        state = state_ref[hd]
        o = _dot(scores.astype(BF16), v) + _dot((q * qdec_ref[:, qk_sl]).astype(BF16), state.astype(BF16))
        k_dec_t = (k * kdec_ref[:, qk_sl]).T.astype(BF16)
        state_ref[hd] = state * chunk_decay[hd] + _dot(k_dec_t, v)
        mu = jnp.mean(o, axis=-1, keepdims=True)
        oc = o - mu
        var = jnp.mean(oc * oc, axis=-1, keepdims=True)
        o = oc * lax.rsqrt(var + EPS) * gn_ref[:, v_sl]
        o_ref[:, v_sl] = (o * _silu(rg_ref[:, v_sl].astype(F32))).astype(BF16)


def _retention(proj, cos_t, sin_t, ret_gn, batch, seq):
    c = RET_CHUNK
    nc = seq // c
    log_gamma = np.log1p(-np.exp2(-5.0 - np.arange(RET_HEADS, dtype=np.float64)))
    chunk_decay = tuple(float(np.exp(c * lg)) for lg in log_gamma)
    lg = jnp.asarray(log_gamma, F32)
    idx = jnp.arange(c, dtype=F32)
    rel = idx[:, None] - idx[None, :]
    causal = rel >= 0
    dmask = jnp.where(causal[None], jnp.exp(jnp.where(causal, rel, 0.0)[None] * lg[:, None, None]), 0.0)
    qdec = jnp.repeat(jnp.exp((idx + 1.0)[:, None] * lg[None, :]), RET_DK, axis=1)
    kdec = jnp.repeat(jnp.exp((c - 1.0 - idx)[:, None] * lg[None, :]), RET_DK, axis=1)
    row = lambda b, t: b * nc + t
    return pl.pallas_call(
        functools.partial(_ret_kernel, chunk_decay=chunk_decay),
        grid=(batch, nc),
        in_specs=[
            pl.BlockSpec((c, RET_QK), lambda b, t: (row(b, t), 0)),
            pl.BlockSpec((c, RET_QK), lambda b, t: (row(b, t), 1)),
            pl.BlockSpec((c, RET_V), lambda b, t: (row(b, t), _BLK_RV)),
            pl.BlockSpec((c, RET_V), lambda b, t: (row(b, t), _BLK_RG)),
            pl.BlockSpec((c, RET_DK), lambda b, t: (t, 0)),
            pl.BlockSpec((c, RET_DK), lambda b, t: (t, 0)),
            pl.BlockSpec((RET_HEADS, c, c), lambda b, t: (0, 0, 0)),
            pl.BlockSpec((c, RET_QK), lambda b, t: (0, 0)),
            pl.BlockSpec((c, RET_QK), lambda b, t: (0, 0)),
            pl.BlockSpec((1, RET_V), lambda b, t: (0, 0)),
        ],
        out_specs=pl.BlockSpec((c, RET_V), lambda b, t: (row(b, t), 0)),
        out_shape=jax.ShapeDtypeStruct((batch * seq, RET_V), BF16),
        scratch_shapes=[pltpu.VMEM((RET_HEADS, RET_DK, RET_DV), F32)],
        compiler_params=_params("parallel", "arbitrary"),
        name="retention",
    )(proj, proj, proj, proj, cos_t, sin_t, dmask, qdec, kdec, ret_gn)


def _gdn_prepare_pieces(slot, r0, gq_ref, gk_ref, gv_ref, gab_ref, cw_ref, shift_ref, alog_ref, dtb_ref, buf_ref,
                        qkv_ref, gate_ref):
    c = GDN_CHUNK

    def column_block(j):
        width = 2 * LANES
        src_ref = (gq_ref, gk_ref, gv_ref)[j * width // GDN_QK]
        src_sl = slice(j * width % GDN_QK, j * width % GDN_QK + width)
        sl = slice(j * width, (j + 1) * width)
        buf_ref[BF16_TILE:BF16_TILE + c, sl] = src_ref[r0:r0 + c, src_sl]
        shifted = _dot(shift_ref[...], buf_ref[:, sl])
        buf_ref[0:BF16_TILE, sl] = buf_ref[c:c + BF16_TILE, sl]
        y = shifted[0:c] * cw_ref[CONV_WIDTH - 1:CONV_WIDTH, sl]
        for s in range(1, CONV_WIDTH):
            y = y + shifted[s * c:(s + 1) * c] * cw_ref[CONV_WIDTH - 1 - s:CONV_WIDTH - s, sl]
        y = _silu(y)
        for half in range(2):
            hsl = slice(half * LANES, (half + 1) * LANES)
            yh = y[:, hsl]
            if sl.start < 2 * GDN_QK:
                scale = GDN_DK ** -0.5 if sl.start < GDN_QK else 1.0
                yh = yh * (lax.rsqrt(jnp.sum(yh * yh, axis=-1, keepdims=True) + EPS) * scale)
            qkv_ref[slot, :, sl.start + half * LANES:sl.start + (half + 1) * LANES] = yh

    def gates():
        gab = gab_ref[r0:r0 + c, :]
        z = gab + dtb_ref[...]
        softplus = jnp.maximum(z, 0.0) + jnp.log(1.0 + jnp.exp(-jnp.abs(z)))
        gc = -jnp.exp(alog_ref[...]) * softplus
        row = lax.broadcasted_iota(jnp.int32, (c, LANES), 0)
        for s in (1, 2, 4, 8, 16, 32):
            gc = gc + jnp.where(row >= s, pltpu.roll(gc, s, 0), 0.0)
        gate_ref[slot, 0] = gc
        gate_ref[slot, 1] = jax.nn.sigmoid(gab)

    return [functools.partial(column_block, j) for j in range(GDN_CONV_DIM // (2 * LANES))] + [gates]


def _gdn_delta_rule_stages(slot, r0, gz_ref, gn_ref, o_ref, state_ref, qkv_ref, gate_ref):
    c = GDN_CHUNK
    gc = gate_ref[slot, 0]
    beta = gate_ref[slot, 1]
    gc_t = gc.T
    ii = lax.broadcasted_iota(jnp.int32, (c, c), 0)
    jj = lax.broadcasted_iota(jnp.int32, (c, c), 1)
    incl = ii >= jj
    strict = ii > jj
    gn = gn_ref[...]
    hs = range(GDN_HEADS)
    sls = [slice(hd * GDN_DK, (hd + 1) * GDN_DK) for hd in hs]
    q_l = [qkv_ref[slot, :, sls[hd]] for hd in hs]
    k_l = [qkv_ref[slot, :, GDN_QK + hd * GDN_DK:GDN_QK + (hd + 1) * GDN_DK] for hd in hs]
    v_l = [qkv_ref[slot, :, 2 * GDN_QK + hd * GDN_DV:2 * GDN_QK + (hd + 1) * GDN_DV] for hd in hs]
    gcol_l = [gc[:, hd:hd + 1] for hd in hs]
    glast_l = [gc[c - 1:c, hd:hd + 1] for hd in hs]
    bcol_l = [beta[:, GDN_HEADS + hd:GDN_HEADS + hd + 1] for hd in hs]
    m_l, x_l, qk_l, ecol_l = [], [], [], []
    for hd in hs:
        decay = jnp.where(incl, jnp.exp(jnp.where(incl, gcol_l[hd] - gc_t[hd:hd + 1, :], 0.0)), 0.0)
        kb = k_l[hd] * bcol_l[hd]
        both = _dot_nt(jnp.concatenate([kb, q_l[hd]], axis=0).astype(BF16), k_l[hd].astype(BF16))
        m_l.append(-(both[0:c] * jnp.where(strict, decay, 0.0)))
        qk_l.append(both[c:2 * c] * decay)
        e_col = jnp.exp(gcol_l[hd])
        ecol_l.append(e_col)
        x_l.append(jnp.concatenate([v_l[hd] * bcol_l[hd], kb * e_col], axis=1))
    yield
    eye = (ii == jj).astype(F32)
    t_l = [eye + m for m in m_l]
    mb_l = [m.astype(BF16) for m in m_l]
    m_l = [_dot(mb_l[hd], mb_l[hd]) for hd in hs]
    for it in range(1, 6):
        mb_l = [m.astype(BF16) for m in m_l]
        if it < 5:
            prod = [_dot(jnp.concatenate([m_l[hd], t_l[hd]], axis=0).astype(BF16), mb_l[hd]) for hd in hs]
            m_next = [pr[0:c] for pr in prod]
            t_l = [t_l[hd] + prod[hd][c:2 * c] for hd in hs]
            m_l = m_next
        else:
            t_l = [t_l[hd] + _dot(t_l[hd].astype(BF16), mb_l[hd]) for hd in hs]
        yield
    x_l = [_dot(t_l[hd].astype(BF16), x_l[hd].astype(BF16)) for hd in hs]
    yield
    st_l = [state_ref[hd] for hd in hs]
    ws_l = [_dot(jnp.concatenate([x_l[hd][:, GDN_DV:], q_l[hd] * ecol_l[hd]], axis=0).astype(BF16),
                 st_l[hd].astype(BF16)) for hd in hs]
    vn_l = [(x_l[hd][:, 0:GDN_DV] - ws_l[hd][0:c]).astype(BF16) for hd in hs]
    yield
    o_l = [ws_l[hd][c:2 * c] + _dot(qk_l[hd].astype(BF16), vn_l[hd]) for hd in hs]
    for hd in hs:
        k_dec_t = (k_l[hd] * jnp.exp(glast_l[hd] - gcol_l[hd])).T.astype(BF16)
        state_ref[hd] = st_l[hd] * jnp.exp(glast_l[hd]) + _dot(k_dec_t, vn_l[hd])
    yield
    for hd in hs:
        o = o_l[hd]
        o = o * lax.rsqrt(jnp.mean(o * o, axis=-1, keepdims=True) + EPS) * gn
        o_ref[r0:r0 + c, sls[hd]] = (o * _silu(gz_ref[r0:r0 + c, sls[hd]].astype(F32))).astype(BF16)


def _gdn_kernel(gq_ref, gk_ref, gv_ref, gz_ref, gab_ref, cw_ref, shift_ref, alog_ref, dtb_ref, gn_ref,
                o_ref, state_ref, buf_ref, qkv_ref, gate_ref):
    @pl.when(pl.program_id(1) == 0)
    def _():
        state_ref[...] = jnp.zeros_like(state_ref)
        buf_ref[0:BF16_TILE, :] = jnp.zeros((BF16_TILE, GDN_CONV_DIM), BF16)
        qkv_ref[...] = jnp.zeros_like(qkv_ref)
        gate_ref[...] = jnp.zeros_like(gate_ref)

    per_stage = 2
    for sub in range(GDN_STEP_CHUNKS):
        r0 = sub * GDN_CHUNK
        pieces = _gdn_prepare_pieces(sub, r0, gq_ref, gk_ref, gv_ref, gab_ref, cw_ref, shift_ref, alog_ref,
                                     dtb_ref, buf_ref, qkv_ref, gate_ref)
        for _ in _gdn_delta_rule_stages(sub, r0, gz_ref, gn_ref, o_ref, state_ref, qkv_ref, gate_ref):
            for piece in pieces[:per_stage]:
                piece()
            pieces = pieces[per_stage:]
        for piece in pieces:
            piece()


def _gdn(proj, gab, conv_w, a_log, dt_bias, gdn_gn, batch, seq):
    c = GDN_CHUNK
    rows = GDN_STEP_CHUNKS * c
    nb = seq // rows
    row = lambda b, t: b * nb + jnp.minimum(t, nb - 1)
    prev = lambda b, t: b * nb + jnp.maximum(t - 1, 0)
    pad = lambda a: jnp.pad(a.reshape(1, -1), ((0, 0), (0, LANES - a.shape[-1])))
    shift = np.zeros((CONV_WIDTH * c, BF16_TILE + c), np.float32)
    for s in range(CONV_WIDTH):
        shift[s * c + np.arange(c), BF16_TILE + np.arange(c) - s] = 1.0
    return pl.pallas_call(
        _gdn_kernel,
        grid=(batch, nb + 1),
        in_specs=[
            pl.BlockSpec((rows, GDN_QK), lambda b, t: (row(b, t), _BLK_GQ)),
            pl.BlockSpec((rows, GDN_QK), lambda b, t: (row(b, t), _BLK_GK)),
            pl.BlockSpec((rows, GDN_V), lambda b, t: (row(b, t), _BLK_GV)),
            pl.BlockSpec((rows, GDN_V), lambda b, t: (prev(b, t), _BLK_GZ)),
            pl.BlockSpec((rows, LANES), lambda b, t: (row(b, t), 0)),
            pl.BlockSpec((CONV_WIDTH, GDN_CONV_DIM), lambda b, t: (0, 0)),
            pl.BlockSpec((CONV_WIDTH * c, BF16_TILE + c), lambda b, t: (0, 0)),
            pl.BlockSpec((1, LANES), lambda b, t: (0, 0)),
            pl.BlockSpec((1, LANES), lambda b, t: (0, 0)),
            pl.BlockSpec((1, GDN_DV), lambda b, t: (0, 0)),
        ],
        out_specs=pl.BlockSpec((rows, GDN_V), lambda b, t: (prev(b, t), 0)),
        out_shape=jax.ShapeDtypeStruct((batch * seq, GDN_V), BF16),
        scratch_shapes=[
            pltpu.VMEM((GDN_HEADS, GDN_DK, GDN_DV), F32),
            pltpu.VMEM((BF16_TILE + c, GDN_CONV_DIM), BF16),
            pltpu.VMEM((GDN_STEP_CHUNKS, c, GDN_CONV_DIM), F32),
            pltpu.VMEM((GDN_STEP_CHUNKS, 2, c, LANES), F32),
        ],
        compiler_params=_params("parallel", "arbitrary"),
        name="gdn",
    )(proj, proj, proj, proj, gab, conv_w, jnp.asarray(shift, BF16), pad(a_log), pad(dt_bias), gdn_gn.reshape(1, -1))


def _mix_out_kernel(h_ref, oret_ref, ogdn_ref, mgr_ref, mgg_ref, wbr_ref, wbg_ref, wo_ref, out_ref):
    y_ret = _dot(oret_ref[...], wbr_ref[...])
    y_gdn = _dot(ogdn_ref[...], wbg_ref[...])
    merged = (jax.nn.sigmoid(mgr_ref[...].astype(F32)) * y_ret
              + jax.nn.sigmoid(mgg_ref[...].astype(F32)) * y_gdn)
    out_ref[...] = h_ref[...] + _dot(merged.astype(BF16), wo_ref[...])


def _mix_out(h, o_ret, o_gdn, proj, w_br_ret, w_br_gdn, w_out, tm=512):
    n = h.shape[0]
    tile = lambda blk: pl.BlockSpec((tm, D_MODEL), lambda i: (i, blk))
    weight = pl.BlockSpec((D_MODEL, D_MODEL), lambda i: (0, 0))
    return pl.pallas_call(
        _mix_out_kernel,
        grid=(n // tm,),
        in_specs=[tile(0), tile(0), tile(0), tile(_BLK_MGR), tile(_BLK_MGG), weight, weight, weight],
        out_specs=tile(0),
        out_shape=jax.ShapeDtypeStruct((n, D_MODEL), F32),
        compiler_params=_params("parallel"),
        name="mix_out",
    )(h, o_ret, o_gdn, proj, proj, w_br_ret, w_br_gdn, w_out)


def _ple_tail(h1, p_ref, gple_ref, wple_ref, wpg_ref, gfin_ref, out_ref, final):
    ple = _dot(p_ref[0].astype(BF16), wple_ref[...])
    gate = jax.nn.sigmoid(_dot(_rms(h1, gple_ref[...]).astype(BF16), wpg_ref[...]))
    h2 = h1 + gate * ple
    out_ref[...] = _rms(h2, gfin_ref[...]) if final else h2


def _ffn_kernel(h_ref, g_ref, wg_ref, wu_ref, wd_ref, p_ref, gple_ref, wple_ref, wpg_ref, gfin_ref,
                out_ref, *, final, tf):
    h = h_ref[...]
    u = _rms(h, g_ref[...]).astype(BF16)
    for j in range(D_FF // tf):
        cols = slice(j * tf, (j + 1) * tf)
        act = _silu(_dot(u, wg_ref[:, cols])) * _dot(u, wu_ref[:, cols])
        h = h + _dot(act.astype(BF16), wd_ref[cols, :])
    _ple_tail(h, p_ref, gple_ref, wple_ref, wpg_ref, gfin_ref, out_ref, final)


def _ffn(h, g, wg, wu, wd, p, layer, g_ple, w_ple, w_ple_gate, g_final, final, tm=512, tf=D_FF // 2):
    n = h.shape[0]
    return pl.pallas_call(
        functools.partial(_ffn_kernel, final=final, tf=tf),
        grid=(n // tm,),
        in_specs=[
            pl.BlockSpec((tm, D_MODEL), lambda i: (i, 0)),
            _resident((1, D_MODEL)),
            _resident((D_MODEL, D_FF)),
            _resident((D_MODEL, D_FF)),
            _resident((D_FF, D_MODEL)),
            pl.BlockSpec((1, tm, PLE_DIM), lambda i: (layer, i, 0)),
            _resident((1, D_MODEL)),
            _resident((PLE_DIM, D_MODEL)),
            _resident((D_MODEL, D_MODEL)),
            _resident((1, D_MODEL)),
        ],
        out_specs=pl.BlockSpec((tm, D_MODEL), lambda i: (i, 0)),
        out_shape=jax.ShapeDtypeStruct((n, D_MODEL), F32),
        compiler_params=_params("parallel"),
        name="ffn_ple",
    )(h, g, wg, wu, wd, p, g_ple, w_ple, w_ple_gate, g_final)


def _router_kernel(h_ref, g_ref, wr_ref, route_ref):
    u = _rms(h_ref[...], g_ref[...])
    w = wr_ref[...]
    u_hi, w_hi = u.astype(BF16), w.astype(BF16)
    u_lo = (u - u_hi.astype(F32)).astype(BF16)
    w_lo = (w - w_hi.astype(F32)).astype(BF16)
    logits = _dot(u_hi, w_hi) + (_dot(u_lo, w_hi) + _dot(u_hi, w_lo))
    lane = lax.broadcasted_iota(jnp.int32, logits.shape, 1)
    logits = jnp.where(lane < N_EXPERTS, logits, -jnp.inf)
    m1 = jnp.max(logits, axis=-1, keepdims=True)
    i1 = jnp.min(jnp.where(logits == m1, lane, LANES), axis=-1, keepdims=True)
    rest = jnp.where(lane == i1, -jnp.inf, logits)
    m2 = jnp.max(rest, axis=-1, keepdims=True)
    i2 = jnp.min(jnp.where(rest == m2, lane, LANES), axis=-1, keepdims=True)
    e2 = jnp.exp(m2 - m1)
    w1 = 1.0 / (1.0 + e2)
    route_ref[...] = (jnp.where(lane == 0, i1.astype(F32), 0.0) + jnp.where(lane == 1, i2.astype(F32), 0.0)
                      + jnp.where(lane == 2, w1, 0.0) + jnp.where(lane == 3, e2 * w1, 0.0))


def _router(h, g, w_router, tm=1024):
    n = h.shape[0]
    return pl.pallas_call(
        _router_kernel,
        grid=(n // tm,),
        in_specs=[
            pl.BlockSpec((tm, D_MODEL), lambda i: (i, 0)),
            pl.BlockSpec((1, D_MODEL), lambda i: (0, 0)),
            pl.BlockSpec((D_MODEL, LANES), lambda i: (0, 0)),
        ],
        out_specs=pl.BlockSpec((tm, LANES), lambda i: (i, 0)),
        out_shape=jax.ShapeDtypeStruct((n, LANES), F32),
        compiler_params=_params("parallel"),
        name="router",
    )(h, g, w_router)


def _routing_tables(route, n):
    n_pairs = 2 * n
    fill_bit = 1 << 15
    assert n_pairs <= fill_bit and MOE_TILE <= fill_bit
    expert = route[:, 0:2].astype(jnp.int32).reshape(-1)
    ids = jnp.arange(N_EXPERTS, dtype=jnp.int32)
    counts = jnp.sum((expert[:, None] == ids[None, :]).astype(jnp.int32), axis=0)
    padded = ((counts + MOE_TILE - 1) // MOE_TILE) * MOE_TILE
    keys = (expert << 16) | jnp.arange(n_pairs, dtype=jnp.int32)
    fill_idx = jnp.arange(MOE_TILE, dtype=jnp.int32)[None, :]
    fill_keys = jnp.where(fill_idx < (padded - counts)[:, None], (ids[:, None] << 16) | fill_bit | fill_idx,
                          jnp.iinfo(jnp.int32).max)
    keys = jnp.sort(jnp.concatenate([keys, fill_keys.reshape(-1)]))
    lookahead = jnp.full(((MOE_STAGES - 1) * MOE_TILE,), jnp.iinfo(jnp.int32).max, jnp.int32)
    keys = jnp.concatenate([keys, lookahead])
    low = keys & (2 * fill_bit - 1)
    valid = low < fill_bit
    row_expert = jnp.minimum(keys >> 16, N_EXPERTS - 1)
    spare = n_pairs + row_expert * MOE_TILE + jnp.minimum(low - fill_bit, MOE_TILE - 1)
    src_row = jnp.where(valid, low >> 1, 0)
    dst_row = jnp.where(valid, (low & 1) * n + (low >> 1), spare)
    tile_expert = row_expert[::MOE_TILE]
    n_used = (jnp.sum(padded) // MOE_TILE).reshape(1)
    return src_row, dst_row, tile_expert, n_used


def _rows_start(count, src_of, dst_of, sem, unrolled):
    def one(r):
        pltpu.make_async_copy(src_of(r), dst_of(r), sem).start()
    if unrolled:
        for r in range(count):
            one(r)
    else:
        def body(r, carry):
            one(r)
            return carry
        lax.fori_loop(0, count, body, 0, unroll=8)


def _moe_gemm_kernel(src_row_ref, dst_row_ref, tile_expert_ref, n_used_ref, h_hbm, g_ref, wg_ref, wu_ref,
                     wd_ref, y_hbm, xs_ref, ys_ref, gsem, ssem):
    del tile_expert_ref
    i = pl.program_id(0)
    last = n_used_ref[0] - 1
    rows = pl.ds(0, MOE_TILE)
    slot = lambda tile: tile % MOE_STAGES

    def gather_start(tile, unrolled):
        _rows_start(MOE_TILE, lambda r: h_hbm.at[pl.ds(src_row_ref[tile * MOE_TILE + r], 1)],
                    lambda r: xs_ref.at[slot(tile), pl.ds(r, 1)], gsem.at[slot(tile)], unrolled)

    def gather_wait(tile):
        pltpu.make_async_copy(h_hbm.at[rows], xs_ref.at[slot(tile)], gsem.at[slot(tile)]).wait()

    def scatter_start(tile, unrolled):
        _rows_start(MOE_TILE, lambda r: ys_ref.at[slot(tile), pl.ds(r, 1)],
                    lambda r: y_hbm.at[pl.ds(dst_row_ref[tile * MOE_TILE + r], 1)], ssem.at[slot(tile)], unrolled)

    def scatter_wait(tile):
        pltpu.make_async_copy(ys_ref.at[slot(tile)], y_hbm.at[rows], ssem.at[slot(tile)]).wait()

    def compute(tile):
        u = _rms(xs_ref[slot(tile)], g_ref[...]).astype(BF16)
        act = _silu(_dot(u, wg_ref[0])) * _dot(u, wu_ref[0])
        ys_ref[slot(tile)] = _dot(act.astype(BF16), wd_ref[0])

    @pl.when(i == 0)
    def _():
        for tile in range(MOE_STAGES):
            gather_start(tile, False)
        ys_ref[1] = jnp.zeros((MOE_TILE, D_MODEL), F32)
        n_pairs = y_hbm.shape[0] - N_EXPERTS * MOE_TILE
        for e in range(N_EXPERTS):
            block = pltpu.make_async_copy(ys_ref.at[1], y_hbm.at[pl.ds(n_pairs + e * MOE_TILE, MOE_TILE)],
                                          ssem.at[1])
            block.start()
            block.wait()
        gather_wait(0)
        compute(0)

    @pl.when(jnp.logical_and(i >= 1, i <= last))
    def _():
        gather_wait(i)
        gather_start(i + MOE_STAGES - 1, True)
        scatter_start(i - 1, True)
        compute(i)

    @pl.when(jnp.logical_and(i >= 2, i <= last))
    def _():
        scatter_wait(i - 2)

    @pl.when(i == last)
    def _():
        for ahead in range(1, MOE_STAGES):
            gather_wait(i + ahead)

        @pl.when(i >= 1)
        def _():
            scatter_wait(i - 1)

        scatter_start(i, False)
        scatter_wait(i)


def _moe_gemm(h, g, src_row, dst_row, tile_expert, n_used, wg, wu, wd):
    n = h.shape[0]
    n_tiles = src_row.shape[0] // MOE_TILE - (MOE_STAGES - 1)
    expert_block = lambda i, sr, dr, te, nu: (te[jnp.minimum(i, nu[0] - 1)], 0, 0)
    return pl.pallas_call(
        _moe_gemm_kernel,
        grid_spec=pltpu.PrefetchScalarGridSpec(
            num_scalar_prefetch=4,
            grid=(n_tiles,),
            in_specs=[
                pl.BlockSpec(memory_space=pl.ANY),
                pl.BlockSpec((1, D_MODEL), lambda i, sr, dr, te, nu: (0, 0)),
                pl.BlockSpec((1, D_MODEL, D_FF_EXPERT), expert_block),
                pl.BlockSpec((1, D_MODEL, D_FF_EXPERT), expert_block),
                pl.BlockSpec((1, D_FF_EXPERT, D_MODEL), expert_block),
            ],
            out_specs=pl.BlockSpec(memory_space=pl.ANY),
            scratch_shapes=[
                pltpu.VMEM((MOE_STAGES, MOE_TILE, D_MODEL), F32),
                pltpu.VMEM((MOE_STAGES, MOE_TILE, D_MODEL), F32),
                pltpu.SemaphoreType.DMA((MOE_STAGES,)),
                pltpu.SemaphoreType.DMA((MOE_STAGES,)),
            ],
        ),
        out_shape=jax.ShapeDtypeStruct((2 * n + N_EXPERTS * MOE_TILE, D_MODEL), F32),
        compiler_params=_params("arbitrary"),
        name="moe_gemm",
    )(src_row, dst_row, tile_expert, n_used, h, g, wg, wu, wd)


def _moe_tail_kernel(h_ref, route_ref, y0_ref, y1_ref, p_ref, gple_ref, wple_ref, wpg_ref, gfin_ref, out_ref,
                     *, final):
    route = route_ref[...]
    h1 = h_ref[...] + route[:, 2:3] * y0_ref[...] + route[:, 3:4] * y1_ref[...]
    _ple_tail(h1, p_ref, gple_ref, wple_ref, wpg_ref, gfin_ref, out_ref, final)


def _moe_tail(h, route, y, p, layer, g_ple, w_ple, w_ple_gate, g_final, final, tm=512):
    n = h.shape[0]
    return pl.pallas_call(
        functools.partial(_moe_tail_kernel, final=final),
        grid=(n // tm,),
        in_specs=[
            pl.BlockSpec((tm, D_MODEL), lambda i: (i, 0)),
            pl.BlockSpec((tm, LANES), lambda i: (i, 0)),
            pl.BlockSpec((tm, D_MODEL), lambda i: (i, 0)),
            pl.BlockSpec((tm, D_MODEL), lambda i: (n // tm + i, 0)),
            pl.BlockSpec((1, tm, PLE_DIM), lambda i: (layer, i, 0)),
            _resident((1, D_MODEL)),
            _resident((PLE_DIM, D_MODEL)),
            _resident((D_MODEL, D_MODEL)),
            _resident((1, D_MODEL)),
        ],
        out_specs=pl.BlockSpec((tm, D_MODEL), lambda i: (i, 0)),
        out_shape=jax.ShapeDtypeStruct((n, D_MODEL), F32),
        compiler_params=_params("parallel"),
        name="moe_tail",
    )(h, route, y, y, p, g_ple, w_ple, w_ple_gate, g_final)


def _moe(h, g, w_router, wg, wu, wd, p, layer, g_ple, w_ple, w_ple_gate, g_final, final):
    n = h.shape[0]
    route = _router(h, g, w_router)
    src_row, dst_row, tile_expert, n_used = _routing_tables(route, n)
    y = _moe_gemm(h, g, src_row, dst_row, tile_expert, n_used, wg, wu, wd)
    return _moe_tail(h, route, y, p, layer, g_ple, w_ple, w_ple_gate, g_final, final)


def _rotary_tables(seq):
    half = RET_DK // 2
    pos = jnp.arange(seq, dtype=F32)
    inv_freq = ROPE_THETA ** (-jnp.arange(half, dtype=F32) / half)
    ang = pos[:, None] * inv_freq[None, :]
    cos, sin = jnp.cos(ang), jnp.sin(ang)
    return jnp.concatenate([cos, cos], axis=-1), jnp.concatenate([-sin, sin], axis=-1)


def kernel(x, p, w_in, conv_w, a_log, dt_bias, ret_gn, gdn_gn, w_br_ret, w_br_gdn, w_out, norm_mix,
           norm_ffn, norm_ple, w_ple, w_ple_gate, ffn_w_gate, ffn_w_up, ffn_w_down, router,
           exp_w_gate, exp_w_up, exp_w_down, norm_final):
    batch, seq, _ = x.shape
    depth = w_in.shape[0]
    n = batch * seq
    cos_t, sin_t = _rotary_tables(seq)
    g_final = norm_final.reshape(1, -1)
    h = x.reshape(n, D_MODEL)
    for i in range(depth):
        w_gab = jnp.pad(w_in[i, :, OFF_GA:OFF_MG], ((0, 0), (0, LANES - 2 * GDN_HEADS))).astype(BF16)
        proj, gab = _in_proj(h, norm_mix[i].reshape(1, -1), w_in[i, :, :OFF_GA].astype(BF16),
                             w_in[i, :, OFF_MG:].astype(BF16), w_gab)
        o_ret = _retention(proj, cos_t, sin_t, ret_gn[i].reshape(1, -1), batch, seq)
        o_gdn = _gdn(proj, gab, conv_w[i], a_log[i], dt_bias[i], gdn_gn[i], batch, seq)
        h = _mix_out(h, o_ret, o_gdn, proj, w_br_ret[i].astype(BF16), w_br_gdn[i].astype(BF16),
                     w_out[i].astype(BF16))
        final = i == depth - 1
        tail = (p.reshape(depth, n, PLE_DIM), i, norm_ple[i].reshape(1, -1), w_ple[i].astype(BF16),
                w_ple_gate[i].astype(BF16), g_final, final)
        j = i // 2
        g_ffn = norm_ffn[i].reshape(1, -1)
        if i % 2 == 0:
            h = _ffn(h, g_ffn, ffn_w_gate[j].astype(BF16), ffn_w_up[j].astype(BF16),
                     ffn_w_down[j].astype(BF16), *tail)
        else:
            w_router = jnp.pad(router[j], ((0, 0), (0, LANES - N_EXPERTS)))
            h = _moe(h, g_ffn, w_router, exp_w_gate[j].astype(BF16), exp_w_up[j].astype(BF16),
                     exp_w_down[j].astype(BF16), *tail)
    return h.reshape(batch, seq, D_MODEL)
```

```python
import functools

import numpy as np
import jax
import jax.numpy as jnp
from jax import lax
from jax.experimental import pallas as pl
from jax.experimental.pallas import tpu as pltpu

F32 = jnp.float32
BF16 = jnp.bfloat16

D_MODEL = 1024
PLE_DIM = 256
EPS = 1e-6
ROPE_THETA = 10000.0
RET_HEADS = 4
RET_DK = 128
RET_DV = 256
RET_QK = RET_HEADS * RET_DK
RET_V = RET_HEADS * RET_DV
GDN_HEADS = 8
GDN_DK = 128
GDN_DV = 128
GDN_QK = GDN_HEADS * GDN_DK
GDN_V = GDN_HEADS * GDN_DV
GDN_CONV_DIM = 2 * GDN_QK + GDN_V
CONV_WIDTH = 4
D_FF = 2816
N_EXPERTS = 8
D_FF_EXPERT = 1408
OFF_GA = 2 * RET_QK + 2 * RET_V + GDN_CONV_DIM + GDN_V
OFF_MG = OFF_GA + 2 * GDN_HEADS
N_MAIN = OFF_GA + 2 * D_MODEL
LANES = 128
GDN_CHUNK = 64
GDN_STEP_CHUNKS = 4
RET_CHUNK = 256
RET_STEP_CHUNKS = 2
MOE_TILE = 512
MOE_STAGES = 3
BF16_TILE = 16
VMEM_LIMIT = 56 * 1024 * 1024

_BLK_RV, _BLK_RG, _BLK_GQ, _BLK_GK, _BLK_GV, _BLK_GZ, _BLK_MGR, _BLK_MGG = 1, 2, 3, 4, 5, 6, 7, 8


def _silu(x):
    return x * jax.nn.sigmoid(x)


def _rms(x, g):
    return x * lax.rsqrt(jnp.mean(x * x, axis=-1, keepdims=True) + EPS) * g


def _dot(a, b):
    return jnp.dot(a, b, preferred_element_type=F32)


def _dot_nt(a, b):
    return lax.dot_general(a, b, (((1,), (1,)), ((), ())), preferred_element_type=F32)


def _params(*sem):
    return pltpu.CompilerParams(dimension_semantics=sem, vmem_limit_bytes=VMEM_LIMIT)


def _in_proj_kernel(x_ref, g_ref, wa_ref, wb_ref, wg_ref, om_ref, og_ref, *, tn):
    u = _rms(x_ref[...], g_ref[...]).astype(BF16)
    for j in range(N_MAIN // tn):
        w_ref, k = (wa_ref, j) if j < OFF_GA // tn else (wb_ref, j - OFF_GA // tn)
        om_ref[:, j * tn:(j + 1) * tn] = _dot(u, w_ref[:, k * tn:(k + 1) * tn]).astype(BF16)
    og_ref[...] = _dot(u, wg_ref[...])


def _resident(shape):
    return pl.BlockSpec(shape, lambda *_: (0,) * len(shape), pipeline_mode=pl.Buffered(1))


def _in_proj(h, g, w_a, w_b, w_gab, tm=512, tn=1024):
    n = h.shape[0]
    return pl.pallas_call(
        functools.partial(_in_proj_kernel, tn=tn),
        grid=(n // tm,),
        in_specs=[
            pl.BlockSpec((tm, D_MODEL), lambda i: (i, 0)),
            _resident((1, D_MODEL)),
            _resident((D_MODEL, OFF_GA)),
            _resident((D_MODEL, N_MAIN - OFF_GA)),
            _resident((D_MODEL, LANES)),
        ],
        out_specs=[
            pl.BlockSpec((tm, N_MAIN), lambda i: (i, 0)),
            pl.BlockSpec((tm, LANES), lambda i: (i, 0)),
        ],
        out_shape=[
            jax.ShapeDtypeStruct((n, N_MAIN), BF16),
            jax.ShapeDtypeStruct((n, LANES), F32),
        ],
        compiler_params=_params("parallel"),
        name="in_proj",
    )(h, g, w_a, w_b, w_gab)


def _ret_kernel(q_ref, k_ref, v_ref, rg_ref, cos_ref, sin_ref, dmask_ref, qdec_ref, kdec_ref,
                gn_ref, o_ref, state_ref, *, chunk_decay):
    @pl.when(pl.program_id(1) == 0)
    def _():
        state_ref[...] = jnp.zeros_like(state_ref)

    for sub in range(RET_STEP_CHUNKS):
        rs = slice(sub * RET_CHUNK, (sub + 1) * RET_CHUNK)
        cos = cos_ref[rs, :]
        sin = sin_ref[rs, :]
        for hd in range(RET_HEADS):
            qk_sl = slice(hd * RET_DK, (hd + 1) * RET_DK)
            v_sl = slice(hd * RET_DV, (hd + 1) * RET_DV)
            q = q_ref[rs, qk_sl].astype(F32)
            k = k_ref[rs, qk_sl].astype(F32)
            q = q * cos + pltpu.roll(q, RET_DK // 2, 1) * sin
            k = (k * cos + pltpu.roll(k, RET_DK // 2, 1) * sin) * (RET_DK ** -0.5)
            v = v_ref[rs, v_sl]
            scores = _dot_nt(q.astype(BF16), k.astype(BF16)) * dmask_ref[hd]
            state = state_ref[hd]
            o = _dot(scores.astype(BF16), v) + _dot((q * qdec_ref[:, qk_sl]).astype(BF16), state.astype(BF16))
            k_dec_t = (k * kdec_ref[:, qk_sl]).T.astype(BF16)
            state_ref[hd] = state * chunk_decay[hd] + _dot(k_dec_t, v)
            mu = jnp.mean(o, axis=-1, keepdims=True)
            oc = o - mu
            var = jnp.mean(oc * oc, axis=-1, keepdims=True)
            o = oc * lax.rsqrt(var + EPS) * gn_ref[:, v_sl]
            o_ref[rs, v_sl] = (o * _silu(rg_ref[rs, v_sl].astype(F32))).astype(BF16)


def _retention(proj, cos_t, sin_t, ret_gn, batch, seq):
    c = RET_CHUNK
    rows = RET_STEP_CHUNKS * c
    nc = seq // rows
    log_gamma = np.log1p(-np.exp2(-5.0 - np.arange(RET_HEADS, dtype=np.float64)))
    chunk_decay = tuple(float(np.exp(c * lg)) for lg in log_gamma)
    lg = jnp.asarray(log_gamma, F32)
    idx = jnp.arange(c, dtype=F32)
    rel = idx[:, None] - idx[None, :]
    causal = rel >= 0
    dmask = jnp.where(causal[None], jnp.exp(jnp.where(causal, rel, 0.0)[None] * lg[:, None, None]), 0.0)
    qdec = jnp.repeat(jnp.exp((idx + 1.0)[:, None] * lg[None, :]), RET_DK, axis=1)
    kdec = jnp.repeat(jnp.exp((c - 1.0 - idx)[:, None] * lg[None, :]), RET_DK, axis=1)
    row = lambda b, t: b * nc + t
    return pl.pallas_call(
        functools.partial(_ret_kernel, chunk_decay=chunk_decay),
        grid=(batch, nc),
        in_specs=[
            pl.BlockSpec((rows, RET_QK), lambda b, t: (row(b, t), 0)),
            pl.BlockSpec((rows, RET_QK), lambda b, t: (row(b, t), 1)),
            pl.BlockSpec((rows, RET_V), lambda b, t: (row(b, t), _BLK_RV)),
            pl.BlockSpec((rows, RET_V), lambda b, t: (row(b, t), _BLK_RG)),
            pl.BlockSpec((rows, RET_DK), lambda b, t: (t, 0)),
            pl.BlockSpec((rows, RET_DK), lambda b, t: (t, 0)),
            pl.BlockSpec((RET_HEADS, c, c), lambda b, t: (0, 0, 0)),
            pl.BlockSpec((c, RET_QK), lambda b, t: (0, 0)),
            pl.BlockSpec((c, RET_QK), lambda b, t: (0, 0)),
            pl.BlockSpec((1, RET_V), lambda b, t: (0, 0)),
        ],
        out_specs=pl.BlockSpec((rows, RET_V), lambda b, t: (row(b, t), 0)),
        out_shape=jax.ShapeDtypeStruct((batch * seq, RET_V), BF16),
        scratch_shapes=[pltpu.VMEM((RET_HEADS, RET_DK, RET_DV), F32)],
        compiler_params=_params("parallel", "arbitrary"),
        name="retention",
    )(proj, proj, proj, proj, cos_t, sin_t, dmask, qdec, kdec, ret_gn)


def _gdn_prepare_pieces(slot, r0, gq_ref, gk_ref, gv_ref, gab_ref, cw_ref, shift_ref, alog_ref, dtb_ref, buf_ref,
                        qkv_ref, gate_ref):
    c = GDN_CHUNK

    def column_block(j):
        width = 2 * LANES
        src_ref = (gq_ref, gk_ref, gv_ref)[j * width // GDN_QK]
        src_sl = slice(j * width % GDN_QK, j * width % GDN_QK + width)
        sl = slice(j * width, (j + 1) * width)
        buf_ref[BF16_TILE:BF16_TILE + c, sl] = src_ref[r0:r0 + c, src_sl]
        shifted = _dot(shift_ref[...], buf_ref[:, sl])
        buf_ref[0:BF16_TILE, sl] = buf_ref[c:c + BF16_TILE, sl]
        y = shifted[0:c] * cw_ref[CONV_WIDTH - 1:CONV_WIDTH, sl]
        for s in range(1, CONV_WIDTH):
            y = y + shifted[s * c:(s + 1) * c] * cw_ref[CONV_WIDTH - 1 - s:CONV_WIDTH - s, sl]
        y = _silu(y)
        for half in range(2):
            hsl = slice(half * LANES, (half + 1) * LANES)
            yh = y[:, hsl]
            if sl.start < 2 * GDN_QK:
                scale = GDN_DK ** -0.5 if sl.start < GDN_QK else 1.0
                yh = yh * (lax.rsqrt(jnp.sum(yh * yh, axis=-1, keepdims=True) + EPS) * scale)
            qkv_ref[slot, :, sl.start + half * LANES:sl.start + (half + 1) * LANES] = yh

    def gates():
        gab = gab_ref[r0:r0 + c, :]
        z = gab + dtb_ref[...]
        softplus = jnp.maximum(z, 0.0) + jnp.log(1.0 + jnp.exp(-jnp.abs(z)))
        gc = -jnp.exp(alog_ref[...]) * softplus
        row = lax.broadcasted_iota(jnp.int32, (c, LANES), 0)
        for s in (1, 2, 4, 8, 16, 32):
            gc = gc + jnp.where(row >= s, pltpu.roll(gc, s, 0), 0.0)
        gate_ref[slot, 0] = gc
        gate_ref[slot, 1] = jax.nn.sigmoid(gab)

    return [functools.partial(column_block, j) for j in range(GDN_CONV_DIM // (2 * LANES))] + [gates]


def _gdn_delta_rule_stages(slot, r0, gz_ref, gn_ref, o_ref, state_ref, qkv_ref, gate_ref):
    c = GDN_CHUNK
    gc = gate_ref[slot, 0]
    beta = gate_ref[slot, 1]
    gc_t = gc.T
    ii = lax.broadcasted_iota(jnp.int32, (c, c), 0)
    jj = lax.broadcasted_iota(jnp.int32, (c, c), 1)
    incl = ii >= jj
    strict = ii > jj
    gn = gn_ref[...]
    hs = range(GDN_HEADS)
    sls = [slice(hd * GDN_DK, (hd + 1) * GDN_DK) for hd in hs]
    q_l = [qkv_ref[slot, :, sls[hd]] for hd in hs]
    k_l = [qkv_ref[slot, :, GDN_QK + hd * GDN_DK:GDN_QK + (hd + 1) * GDN_DK] for hd in hs]
    v_l = [qkv_ref[slot, :, 2 * GDN_QK + hd * GDN_DV:2 * GDN_QK + (hd + 1) * GDN_DV] for hd in hs]
    gcol_l = [gc[:, hd:hd + 1] for hd in hs]
    glast_l = [gc[c - 1:c, hd:hd + 1] for hd in hs]
    bcol_l = [beta[:, GDN_HEADS + hd:GDN_HEADS + hd + 1] for hd in hs]
    m_l, x_l, qk_l, ecol_l = [], [], [], []
    for hd in hs:
        decay = jnp.where(incl, jnp.exp(jnp.where(incl, gcol_l[hd] - gc_t[hd:hd + 1, :], 0.0)), 0.0)
        kb = k_l[hd] * bcol_l[hd]
        both = _dot_nt(jnp.concatenate([kb, q_l[hd]], axis=0).astype(BF16), k_l[hd].astype(BF16))
        m_l.append(-(both[0:c] * jnp.where(strict, decay, 0.0)))
        qk_l.append(both[c:2 * c] * decay)
        e_col = jnp.exp(gcol_l[hd])
        ecol_l.append(e_col)
        x_l.append(jnp.concatenate([v_l[hd] * bcol_l[hd], kb * e_col], axis=1))
    yield
    eye = (ii == jj).astype(F32)
    t_l = [eye + m for m in m_l]
    mb_l = [m.astype(BF16) for m in m_l]
    m_l = [_dot(mb_l[hd], mb_l[hd]) for hd in hs]
    for it in range(1, 6):
        mb_l = [m.astype(BF16) for m in m_l]
        if it < 5:
            prod = [_dot(jnp.concatenate([m_l[hd], t_l[hd]], axis=0).astype(BF16), mb_l[hd]) for hd in hs]
            m_next = [pr[0:c] for pr in prod]
            t_l = [t_l[hd] + prod[hd][c:2 * c] for hd in hs]
            m_l = m_next
        else:
            t_l = [t_l[hd] + _dot(t_l[hd].astype(BF16), mb_l[hd]) for hd in hs]
        yield
    x_l = [_dot(t_l[hd].astype(BF16), x_l[hd].astype(BF16)) for hd in hs]
    yield
    st_l = [state_ref[hd] for hd in hs]
    ws_l = [_dot(jnp.concatenate([x_l[hd][:, GDN_DV:], q_l[hd] * ecol_l[hd]], axis=0).astype(BF16),
                 st_l[hd].astype(BF16)) for hd in hs]
    vn_l = [(x_l[hd][:, 0:GDN_DV] - ws_l[hd][0:c]).astype(BF16) for hd in hs]
    yield
    o_l = [ws_l[hd][c:2 * c] + _dot(qk_l[hd].astype(BF16), vn_l[hd]) for hd in hs]
    for hd in hs:
        k_dec_t = (k_l[hd] * jnp.exp(glast_l[hd] - gcol_l[hd])).T.astype(BF16)
        state_ref[hd] = st_l[hd] * jnp.exp(glast_l[hd]) + _dot(k_dec_t, vn_l[hd])
    yield
    for hd in hs:
        o = o_l[hd]
        o = o * lax.rsqrt(jnp.mean(o * o, axis=-1, keepdims=True) + EPS) * gn
        o_ref[r0:r0 + c, sls[hd]] = (o * _silu(gz_ref[r0:r0 + c, sls[hd]].astype(F32))).astype(BF16)


def _gdn_kernel(gq_ref, gk_ref, gv_ref, gz_ref, gab_ref, cw_ref, shift_ref, alog_ref, dtb_ref, gn_ref,
                o_ref, state_ref, buf_ref, qkv_ref, gate_ref):
    @pl.when(pl.program_id(1) == 0)
    def _():
        state_ref[...] = jnp.zeros_like(state_ref)
        buf_ref[0:BF16_TILE, :] = jnp.zeros((BF16_TILE, GDN_CONV_DIM), BF16)
        qkv_ref[...] = jnp.zeros_like(qkv_ref)
        gate_ref[...] = jnp.zeros_like(gate_ref)

    per_stage = 2
    for sub in range(GDN_STEP_CHUNKS):
        r0 = sub * GDN_CHUNK
        pieces = _gdn_prepare_pieces(sub, r0, gq_ref, gk_ref, gv_ref, gab_ref, cw_ref, shift_ref, alog_ref,
                                     dtb_ref, buf_ref, qkv_ref, gate_ref)
        for _ in _gdn_delta_rule_stages(sub, r0, gz_ref, gn_ref, o_ref, state_ref, qkv_ref, gate_ref):
            for piece in pieces[:per_stage]:
                piece()
            pieces = pieces[per_stage:]
        for piece in pieces:
            piece()


def _gdn(proj, gab, conv_w, a_log, dt_bias, gdn_gn, batch, seq):
    c = GDN_CHUNK
    rows = GDN_STEP_CHUNKS * c
    nb = seq // rows
    row = lambda b, t: b * nb + jnp.minimum(t, nb - 1)
    prev = lambda b, t: b * nb + jnp.maximum(t - 1, 0)
    pad = lambda a: jnp.pad(a.reshape(1, -1), ((0, 0), (0, LANES - a.shape[-1])))
    shift = np.zeros((CONV_WIDTH * c, BF16_TILE + c), np.float32)
    for s in range(CONV_WIDTH):
        shift[s * c + np.arange(c), BF16_TILE + np.arange(c) - s] = 1.0
    return pl.pallas_call(
        _gdn_kernel,
        grid=(batch, nb + 1),
        in_specs=[
            pl.BlockSpec((rows, GDN_QK), lambda b, t: (row(b, t), _BLK_GQ)),
            pl.BlockSpec((rows, GDN_QK), lambda b, t: (row(b, t), _BLK_GK)),
            pl.BlockSpec((rows, GDN_V), lambda b, t: (row(b, t), _BLK_GV)),
            pl.BlockSpec((rows, GDN_V), lambda b, t: (prev(b, t), _BLK_GZ)),
            pl.BlockSpec((rows, LANES), lambda b, t: (row(b, t), 0)),
            pl.BlockSpec((CONV_WIDTH, GDN_CONV_DIM), lambda b, t: (0, 0)),
            pl.BlockSpec((CONV_WIDTH * c, BF16_TILE + c), lambda b, t: (0, 0)),
            pl.BlockSpec((1, LANES), lambda b, t: (0, 0)),
            pl.BlockSpec((1, LANES), lambda b, t: (0, 0)),
            pl.BlockSpec((1, GDN_DV), lambda b, t: (0, 0)),
        ],
        out_specs=pl.BlockSpec((rows, GDN_V), lambda b, t: (prev(b, t), 0)),
        out_shape=jax.ShapeDtypeStruct((batch * seq, GDN_V), BF16),
        scratch_shapes=[
            pltpu.VMEM((GDN_HEADS, GDN_DK, GDN_DV), F32),
            pltpu.VMEM((BF16_TILE + c, GDN_CONV_DIM), BF16),
            pltpu.VMEM((GDN_STEP_CHUNKS, c, GDN_CONV_DIM), F32),
            pltpu.VMEM((GDN_STEP_CHUNKS, 2, c, LANES), F32),
        ],
        compiler_params=_params("parallel", "arbitrary"),
        name="gdn",
    )(proj, proj, proj, proj, gab, conv_w, jnp.asarray(shift, BF16), pad(a_log), pad(dt_bias), gdn_gn.reshape(1, -1))


def _mix_out_kernel(h_ref, oret_ref, ogdn_ref, mgr_ref, mgg_ref, wbr_ref, wbg_ref, wo_ref, out_ref):
    y_ret = _dot(oret_ref[...], wbr_ref[...])
    y_gdn = _dot(ogdn_ref[...], wbg_ref[...])
    merged = (jax.nn.sigmoid(mgr_ref[...].astype(F32)) * y_ret
              + jax.nn.sigmoid(mgg_ref[...].astype(F32)) * y_gdn)
    out_ref[...] = h_ref[...] + _dot(merged.astype(BF16), wo_ref[...])


def _mix_out(h, o_ret, o_gdn, proj, w_br_ret, w_br_gdn, w_out, tm=512):
    n = h.shape[0]
    tile = lambda blk: pl.BlockSpec((tm, D_MODEL), lambda i: (i, blk))
    weight = pl.BlockSpec((D_MODEL, D_MODEL), lambda i: (0, 0))
    return pl.pallas_call(
        _mix_out_kernel,
        grid=(n // tm,),
        in_specs=[tile(0), tile(0), tile(0), tile(_BLK_MGR), tile(_BLK_MGG), weight, weight, weight],
        out_specs=tile(0),
        out_shape=jax.ShapeDtypeStruct((n, D_MODEL), F32),
        compiler_params=_params("parallel"),
        name="mix_out",
    )(h, o_ret, o_gdn, proj, proj, w_br_ret, w_br_gdn, w_out)


def _ple_tail(h1, p_ref, gple_ref, wple_ref, wpg_ref, gfin_ref, out_ref, final):
    ple = _dot(p_ref[0].astype(BF16), wple_ref[...])
    gate = jax.nn.sigmoid(_dot(_rms(h1, gple_ref[...]).astype(BF16), wpg_ref[...]))
    h2 = h1 + gate * ple
    out_ref[...] = _rms(h2, gfin_ref[...]) if final else h2


def _ffn_kernel(h_ref, g_ref, wg_ref, wu_ref, wd_ref, p_ref, gple_ref, wple_ref, wpg_ref, gfin_ref,
                out_ref, *, final, tf):
    h = h_ref[...]
    u = _rms(h, g_ref[...]).astype(BF16)
    for j in range(D_FF // tf):
        cols = slice(j * tf, (j + 1) * tf)
        act = _silu(_dot(u, wg_ref[:, cols])) * _dot(u, wu_ref[:, cols])
        h = h + _dot(act.astype(BF16), wd_ref[cols, :])
    _ple_tail(h, p_ref, gple_ref, wple_ref, wpg_ref, gfin_ref, out_ref, final)


def _ffn(h, g, wg, wu, wd, p, layer, g_ple, w_ple, w_ple_gate, g_final, final, tm=512, tf=256):
    n = h.shape[0]
    return pl.pallas_call(
        functools.partial(_ffn_kernel, final=final, tf=tf),
        grid=(n // tm,),
        in_specs=[
            pl.BlockSpec((tm, D_MODEL), lambda i: (i, 0)),
            _resident((1, D_MODEL)),
            _resident((D_MODEL, D_FF)),
            _resident((D_MODEL, D_FF)),
            _resident((D_FF, D_MODEL)),
            pl.BlockSpec((1, tm, PLE_DIM), lambda i: (layer, i, 0)),
            _resident((1, D_MODEL)),
            _resident((PLE_DIM, D_MODEL)),
            _resident((D_MODEL, D_MODEL)),
            _resident((1, D_MODEL)),
        ],
        out_specs=pl.BlockSpec((tm, D_MODEL), lambda i: (i, 0)),
        out_shape=jax.ShapeDtypeStruct((n, D_MODEL), F32),
        compiler_params=_params("parallel"),
        name="ffn_ple",
    )(h, g, wg, wu, wd, p, g_ple, w_ple, w_ple_gate, g_final)


def _router_kernel(h_ref, g_ref, wr_ref, route_ref):
    u = _rms(h_ref[...], g_ref[...])
    w = wr_ref[...]
    u_hi, w_hi = u.astype(BF16), w.astype(BF16)
    u_lo = (u - u_hi.astype(F32)).astype(BF16)
    w_lo = (w - w_hi.astype(F32)).astype(BF16)
    logits = _dot(u_hi, w_hi) + (_dot(u_lo, w_hi) + _dot(u_hi, w_lo))
    lane = lax.broadcasted_iota(jnp.int32, logits.shape, 1)
    logits = jnp.where(lane < N_EXPERTS, logits, -jnp.inf)
    m1 = jnp.max(logits, axis=-1, keepdims=True)
    i1 = jnp.min(jnp.where(logits == m1, lane, LANES), axis=-1, keepdims=True)
    rest = jnp.where(lane == i1, -jnp.inf, logits)
    m2 = jnp.max(rest, axis=-1, keepdims=True)
    i2 = jnp.min(jnp.where(rest == m2, lane, LANES), axis=-1, keepdims=True)
    e2 = jnp.exp(m2 - m1)
    w1 = 1.0 / (1.0 + e2)
    route_ref[...] = (jnp.where(lane == 0, i1.astype(F32), 0.0) + jnp.where(lane == 1, i2.astype(F32), 0.0)
                      + jnp.where(lane == 2, w1, 0.0) + jnp.where(lane == 3, e2 * w1, 0.0))


def _router(h, g, w_router, tm=1024):
    n = h.shape[0]
    return pl.pallas_call(
        _router_kernel,
        grid=(n // tm,),
        in_specs=[
            pl.BlockSpec((tm, D_MODEL), lambda i: (i, 0)),
            pl.BlockSpec((1, D_MODEL), lambda i: (0, 0)),
            pl.BlockSpec((D_MODEL, LANES), lambda i: (0, 0)),
        ],
        out_specs=pl.BlockSpec((tm, LANES), lambda i: (i, 0)),
        out_shape=jax.ShapeDtypeStruct((n, LANES), F32),
        compiler_params=_params("parallel"),
        name="router",
    )(h, g, w_router)


def _routing_tables(route, n):
    n_pairs = 2 * n
    fill_bit = 1 << 15
    assert n_pairs <= fill_bit and MOE_TILE <= fill_bit
    expert = route[:, 0:2].astype(jnp.int32).reshape(-1)
    ids = jnp.arange(N_EXPERTS, dtype=jnp.int32)
    counts = jnp.sum((expert[:, None] == ids[None, :]).astype(jnp.int32), axis=0)
    padded = ((counts + MOE_TILE - 1) // MOE_TILE) * MOE_TILE
    keys = (expert << 16) | jnp.arange(n_pairs, dtype=jnp.int32)
    fill_idx = jnp.arange(MOE_TILE, dtype=jnp.int32)[None, :]
    fill_keys = jnp.where(fill_idx < (padded - counts)[:, None], (ids[:, None] << 16) | fill_bit | fill_idx,
                          jnp.iinfo(jnp.int32).max)
    keys = jnp.sort(jnp.concatenate([keys, fill_keys.reshape(-1)]))
    lookahead = jnp.full(((MOE_STAGES - 1) * MOE_TILE,), jnp.iinfo(jnp.int32).max, jnp.int32)
    keys = jnp.concatenate([keys, lookahead])
    low = keys & (2 * fill_bit - 1)
    valid = low < fill_bit
    row_expert = jnp.minimum(keys >> 16, N_EXPERTS - 1)
    spare = n_pairs + row_expert * MOE_TILE + jnp.minimum(low - fill_bit, MOE_TILE - 1)
    src_row = jnp.where(valid, low >> 1, 0)
    dst_row = jnp.where(valid, (low & 1) * n + (low >> 1), spare)
    tile_expert = row_expert[::MOE_TILE]
    n_used = (jnp.sum(padded) // MOE_TILE).reshape(1)
    return src_row, dst_row, tile_expert, n_used


def _rows_start(count, src_of, dst_of, sem, unrolled):
    def one(r):
        pltpu.make_async_copy(src_of(r), dst_of(r), sem).start()
    if unrolled:
        for r in range(count):
            one(r)
    else:
        def body(r, carry):
            one(r)
            return carry
        lax.fori_loop(0, count, body, 0, unroll=8)


def _moe_gemm_kernel(src_row_ref, dst_row_ref, tile_expert_ref, n_used_ref, h_hbm, g_ref, wg_ref, wu_ref,
                     wd_ref, y_hbm, xs_ref, ys_ref, gsem, ssem):
    del tile_expert_ref
    i = pl.program_id(0)
    last = n_used_ref[0] - 1
    rows = pl.ds(0, MOE_TILE)
    slot = lambda tile: tile % MOE_STAGES

    def gather_start(tile, unrolled):
        _rows_start(MOE_TILE, lambda r: h_hbm.at[pl.ds(src_row_ref[tile * MOE_TILE + r], 1)],
                    lambda r: xs_ref.at[slot(tile), pl.ds(r, 1)], gsem.at[slot(tile)], unrolled)

    def gather_wait(tile):
        pltpu.make_async_copy(h_hbm.at[rows], xs_ref.at[slot(tile)], gsem.at[slot(tile)]).wait()

    def scatter_start(tile, unrolled):
        _rows_start(MOE_TILE, lambda r: ys_ref.at[slot(tile), pl.ds(r, 1)],
                    lambda r: y_hbm.at[pl.ds(dst_row_ref[tile * MOE_TILE + r], 1)], ssem.at[slot(tile)], unrolled)

    def scatter_wait(tile):
        pltpu.make_async_copy(ys_ref.at[slot(tile)], y_hbm.at[rows], ssem.at[slot(tile)]).wait()

    def compute(tile):
        u = _rms(xs_ref[slot(tile)], g_ref[...]).astype(BF16)
        act = _silu(_dot(u, wg_ref[0])) * _dot(u, wu_ref[0])
        ys_ref[slot(tile)] = _dot(act.astype(BF16), wd_ref[0])

    @pl.when(i == 0)
    def _():
        for tile in range(MOE_STAGES):
            gather_start(tile, False)
        ys_ref[1] = jnp.zeros((MOE_TILE, D_MODEL), F32)
        n_pairs = y_hbm.shape[0] - N_EXPERTS * MOE_TILE
        for e in range(N_EXPERTS):
            block = pltpu.make_async_copy(ys_ref.at[1], y_hbm.at[pl.ds(n_pairs + e * MOE_TILE, MOE_TILE)],
                                          ssem.at[1])
            block.start()
            block.wait()
        gather_wait(0)
        compute(0)

    @pl.when(jnp.logical_and(i >= 1, i <= last))
    def _():
        gather_wait(i)
        gather_start(i + MOE_STAGES - 1, True)
        scatter_start(i - 1, True)
        compute(i)

    @pl.when(jnp.logical_and(i >= 2, i <= last))
    def _():
        scatter_wait(i - 2)

    @pl.when(i == last)
    def _():
        for ahead in range(1, MOE_STAGES):
            gather_wait(i + ahead)

        @pl.when(i >= 1)
        def _():
            scatter_wait(i - 1)

        scatter_start(i, False)
        scatter_wait(i)


def _moe_gemm(h, g, src_row, dst_row, tile_expert, n_used, wg, wu, wd):
    n = h.shape[0]
    n_tiles = src_row.shape[0] // MOE_TILE - (MOE_STAGES - 1)
    expert_block = lambda i, sr, dr, te, nu: (te[jnp.minimum(i, nu[0] - 1)], 0, 0)
    return pl.pallas_call(
        _moe_gemm_kernel,
        grid_spec=pltpu.PrefetchScalarGridSpec(
            num_scalar_prefetch=4,
            grid=(n_tiles,),
            in_specs=[
                pl.BlockSpec(memory_space=pl.ANY),
                pl.BlockSpec((1, D_MODEL), lambda i, sr, dr, te, nu: (0, 0)),
                pl.BlockSpec((1, D_MODEL, D_FF_EXPERT), expert_block),
                pl.BlockSpec((1, D_MODEL, D_FF_EXPERT), expert_block),
                pl.BlockSpec((1, D_FF_EXPERT, D_MODEL), expert_block),
            ],
            out_specs=pl.BlockSpec(memory_space=pl.ANY),
            scratch_shapes=[
                pltpu.VMEM((MOE_STAGES, MOE_TILE, D_MODEL), F32),
                pltpu.VMEM((MOE_STAGES, MOE_TILE, D_MODEL), F32),
                pltpu.SemaphoreType.DMA((MOE_STAGES,)),
                pltpu.SemaphoreType.DMA((MOE_STAGES,)),
            ],
        ),
        out_shape=jax.ShapeDtypeStruct((2 * n + N_EXPERTS * MOE_TILE, D_MODEL), F32),
        compiler_params=_params("arbitrary"),
        name="moe_gemm",
    )(src_row, dst_row, tile_expert, n_used, h, g, wg, wu, wd)


def _moe_tail_kernel(h_ref, route_ref, y0_ref, y1_ref, p_ref, gple_ref, wple_ref, wpg_ref, gfin_ref, out_ref,
                     *, final):
    route = route_ref[...]
    h1 = h_ref[...] + route[:, 2:3] * y0_ref[...] + route[:, 3:4] * y1_ref[...]
    _ple_tail(h1, p_ref, gple_ref, wple_ref, wpg_ref, gfin_ref, out_ref, final)


def _moe_tail(h, route, y, p, layer, g_ple, w_ple, w_ple_gate, g_final, final, tm=512):
    n = h.shape[0]
    return pl.pallas_call(
        functools.partial(_moe_tail_kernel, final=final),
        grid=(n // tm,),
        in_specs=[
            pl.BlockSpec((tm, D_MODEL), lambda i: (i, 0)),
            pl.BlockSpec((tm, LANES), lambda i: (i, 0)),
            pl.BlockSpec((tm, D_MODEL), lambda i: (i, 0)),
            pl.BlockSpec((tm, D_MODEL), lambda i: (n // tm + i, 0)),
            pl.BlockSpec((1, tm, PLE_DIM), lambda i: (layer, i, 0)),
            _resident((1, D_MODEL)),
            _resident((PLE_DIM, D_MODEL)),
            _resident((D_MODEL, D_MODEL)),
            _resident((1, D_MODEL)),
        ],
        out_specs=pl.BlockSpec((tm, D_MODEL), lambda i: (i, 0)),
        out_shape=jax.ShapeDtypeStruct((n, D_MODEL), F32),
        compiler_params=_params("parallel"),
        name="moe_tail",
    )(h, route, y, y, p, g_ple, w_ple, w_ple_gate, g_final)


def _moe(h, g, w_router, wg, wu, wd, p, layer, g_ple, w_ple, w_ple_gate, g_final, final):
    n = h.shape[0]
    route = _router(h, g, w_router)
    src_row, dst_row, tile_expert, n_used = _routing_tables(route, n)
    y = _moe_gemm(h, g, src_row, dst_row, tile_expert, n_used, wg, wu, wd)
    return _moe_tail(h, route, y, p, layer, g_ple, w_ple, w_ple_gate, g_final, final)


def _rotary_tables(seq):
    half = RET_DK // 2
    pos = jnp.arange(seq, dtype=F32)
    inv_freq = ROPE_THETA ** (-jnp.arange(half, dtype=F32) / half)
    ang = pos[:, None] * inv_freq[None, :]
    cos, sin = jnp.cos(ang), jnp.sin(ang)
    return jnp.concatenate([cos, cos], axis=-1), jnp.concatenate([-sin, sin], axis=-1)


def kernel(x, p, w_in, conv_w, a_log, dt_bias, ret_gn, gdn_gn, w_br_ret, w_br_gdn, w_out, norm_mix,
           norm_ffn, norm_ple, w_ple, w_ple_gate, ffn_w_gate, ffn_w_up, ffn_w_down, router,
           exp_w_gate, exp_w_up, exp_w_down, norm_final):
    batch, seq, _ = x.shape
    depth = w_in.shape[0]
    n = batch * seq
    cos_t, sin_t = _rotary_tables(seq)
    g_final = norm_final.reshape(1, -1)
    h = x.reshape(n, D_MODEL)
    for i in range(depth):
        w_gab = jnp.pad(w_in[i, :, OFF_GA:OFF_MG], ((0, 0), (0, LANES - 2 * GDN_HEADS))).astype(BF16)
        proj, gab = _in_proj(h, norm_mix[i].reshape(1, -1), w_in[i, :, :OFF_GA].astype(BF16),
                             w_in[i, :, OFF_MG:].astype(BF16), w_gab)
        o_ret = _retention(proj, cos_t, sin_t, ret_gn[i].reshape(1, -1), batch, seq)
        o_gdn = _gdn(proj, gab, conv_w[i], a_log[i], dt_bias[i], gdn_gn[i], batch, seq)
        h = _mix_out(h, o_ret, o_gdn, proj, w_br_ret[i].astype(BF16), w_br_gdn[i].astype(BF16),
                     w_out[i].astype(BF16))
        final = i == depth - 1
        tail = (p.reshape(depth, n, PLE_DIM), i, norm_ple[i].reshape(1, -1), w_ple[i].astype(BF16),
                w_ple_gate[i].astype(BF16), g_final, final)
        j = i // 2
        g_ffn = norm_ffn[i].reshape(1, -1)
        if i % 2 == 0:
            h = _ffn(h, g_ffn, ffn_w_gate[j].astype(BF16), ffn_w_up[j].astype(BF16),
                     ffn_w_down[j].astype(BF16), *tail)
        else:
            w_router = jnp.pad(router[j], ((0, 0), (0, LANES - N_EXPERTS)))
            h = _moe(h, g_ffn, w_router, exp_w_gate[j].astype(BF16), exp_w_up[j].astype(BF16),
                     exp_w_down[j].astype(BF16), *tail)
    return h.reshape(batch, seq, D_MODEL)
```

```python
import functools

import numpy as np
import jax
import jax.numpy as jnp
from jax import lax
from jax.experimental import pallas as pl
from jax.experimental.pallas import tpu as pltpu

F32 = jnp.float32
BF16 = jnp.bfloat16

D_MODEL = 1024
PLE_DIM = 256
EPS = 1e-6
ROPE_THETA = 10000.0
RET_HEADS = 4
RET_DK = 128
RET_DV = 256
RET_QK = RET_HEADS * RET_DK
RET_V = RET_HEADS * RET_DV
GDN_HEADS = 8
GDN_DK = 128
GDN_DV = 128
GDN_QK = GDN_HEADS * GDN_DK
GDN_V = GDN_HEADS * GDN_DV
GDN_CONV_DIM = 2 * GDN_QK + GDN_V
CONV_WIDTH = 4
D_FF = 2816
N_EXPERTS = 8
D_FF_EXPERT = 1408
OFF_GA = 2 * RET_QK + 2 * RET_V + GDN_CONV_DIM + GDN_V
OFF_MG = OFF_GA + 2 * GDN_HEADS
N_MAIN = OFF_GA + 2 * D_MODEL
LANES = 128
GDN_CHUNK = 64
GDN_STEP_CHUNKS = 8
RET_CHUNK = 256
RET_STEP_CHUNKS = 2
MOE_TILE = 512
MOE_STAGES = 3
BF16_TILE = 16
VMEM_LIMIT = 56 * 1024 * 1024

_BLK_RV, _BLK_RG, _BLK_GQ, _BLK_GK, _BLK_GV, _BLK_GZ, _BLK_MGR, _BLK_MGG = 1, 2, 3, 4, 5, 6, 7, 8


def _silu(x):
    return x * jax.nn.sigmoid(x)


def _rms(x, g):
    return x * lax.rsqrt(jnp.mean(x * x, axis=-1, keepdims=True) + EPS) * g


def _dot(a, b):
    return jnp.dot(a, b, preferred_element_type=F32)


def _dot_nt(a, b):
    return lax.dot_general(a, b, (((1,), (1,)), ((), ())), preferred_element_type=F32)


def _params(*sem):
    return pltpu.CompilerParams(dimension_semantics=sem, vmem_limit_bytes=VMEM_LIMIT)


def _in_proj_kernel(x_ref, g_ref, wa_ref, wb_ref, wg_ref, om_ref, og_ref, *, tn):
    u = _rms(x_ref[...], g_ref[...]).astype(BF16)
    for j in range(N_MAIN // tn):
        w_ref, k = (wa_ref, j) if j < OFF_GA // tn else (wb_ref, j - OFF_GA // tn)
        om_ref[:, j * tn:(j + 1) * tn] = _dot(u, w_ref[:, k * tn:(k + 1) * tn]).astype(BF16)
    og_ref[...] = _dot(u, wg_ref[...])


def _resident(shape):
    return pl.BlockSpec(shape, lambda *_: (0,) * len(shape), pipeline_mode=pl.Buffered(1))


def _in_proj(h, g, w_a, w_b, w_gab, tm=512, tn=1024):
    n = h.shape[0]
    return pl.pallas_call(
        functools.partial(_in_proj_kernel, tn=tn),
        grid=(n // tm,),
        in_specs=[
            pl.BlockSpec((tm, D_MODEL), lambda i: (i, 0)),
            _resident((1, D_MODEL)),
            _resident((D_MODEL, OFF_GA)),
            _resident((D_MODEL, N_MAIN - OFF_GA)),
            _resident((D_MODEL, LANES)),
        ],
        out_specs=[
            pl.BlockSpec((tm, N_MAIN), lambda i: (i, 0)),
            pl.BlockSpec((tm, LANES), lambda i: (i, 0)),
        ],
        out_shape=[
            jax.ShapeDtypeStruct((n, N_MAIN), BF16),
            jax.ShapeDtypeStruct((n, LANES), F32),
        ],
        compiler_params=_params("parallel"),
        name="in_proj",
    )(h, g, w_a, w_b, w_gab)


def _ret_kernel(q_ref, k_ref, v_ref, rg_ref, cos_ref, sin_ref, dmask_ref, qdec_ref, kdec_ref,
                gn_ref, o_ref, state_ref, *, chunk_decay):
    @pl.when(pl.program_id(1) == 0)
    def _():
        state_ref[...] = jnp.zeros_like(state_ref)

    for sub in range(RET_STEP_CHUNKS):
        rs = slice(sub * RET_CHUNK, (sub + 1) * RET_CHUNK)
        cos = cos_ref[rs, :]
        sin = sin_ref[rs, :]
        for hd in range(RET_HEADS):
            qk_sl = slice(hd * RET_DK, (hd + 1) * RET_DK)
            v_sl = slice(hd * RET_DV, (hd + 1) * RET_DV)
            q = q_ref[rs, qk_sl].astype(F32)
            k = k_ref[rs, qk_sl].astype(F32)
            q = q * cos + pltpu.roll(q, RET_DK // 2, 1) * sin
            k = (k * cos + pltpu.roll(k, RET_DK // 2, 1) * sin) * (RET_DK ** -0.5)
            v = v_ref[rs, v_sl]
            scores = _dot_nt(q.astype(BF16), k.astype(BF16)) * dmask_ref[hd]
            state = state_ref[hd]
            o = _dot(scores.astype(BF16), v) + _dot((q * qdec_ref[:, qk_sl]).astype(BF16), state.astype(BF16))
            k_dec_t = (k * kdec_ref[:, qk_sl]).T.astype(BF16)
            state_ref[hd] = state * chunk_decay[hd] + _dot(k_dec_t, v)
            mu = jnp.mean(o, axis=-1, keepdims=True)
            oc = o - mu
            var = jnp.mean(oc * oc, axis=-1, keepdims=True)
            o = oc * lax.rsqrt(var + EPS) * gn_ref[:, v_sl]
            o_ref[rs, v_sl] = (o * _silu(rg_ref[rs, v_sl].astype(F32))).astype(BF16)


def _retention(proj, cos_t, sin_t, ret_gn, batch, seq):
    c = RET_CHUNK
    rows = RET_STEP_CHUNKS * c
    nc = seq // rows
    log_gamma = np.log1p(-np.exp2(-5.0 - np.arange(RET_HEADS, dtype=np.float64)))
    chunk_decay = tuple(float(np.exp(c * lg)) for lg in log_gamma)
    lg = jnp.asarray(log_gamma, F32)
    idx = jnp.arange(c, dtype=F32)
    rel = idx[:, None] - idx[None, :]
    causal = rel >= 0
    dmask = jnp.where(causal[None], jnp.exp(jnp.where(causal, rel, 0.0)[None] * lg[:, None, None]), 0.0)
    qdec = jnp.repeat(jnp.exp((idx + 1.0)[:, None] * lg[None, :]), RET_DK, axis=1)
    kdec = jnp.repeat(jnp.exp((c - 1.0 - idx)[:, None] * lg[None, :]), RET_DK, axis=1)
    row = lambda b, t: b * nc + t
    return pl.pallas_call(
        functools.partial(_ret_kernel, chunk_decay=chunk_decay),
        grid=(batch, nc),
        in_specs=[
            pl.BlockSpec((rows, RET_QK), lambda b, t: (row(b, t), 0)),
            pl.BlockSpec((rows, RET_QK), lambda b, t: (row(b, t), 1)),
            pl.BlockSpec((rows, RET_V), lambda b, t: (row(b, t), _BLK_RV)),
            pl.BlockSpec((rows, RET_V), lambda b, t: (row(b, t), _BLK_RG)),
            pl.BlockSpec((rows, RET_DK), lambda b, t: (t, 0)),
            pl.BlockSpec((rows, RET_DK), lambda b, t: (t, 0)),
            pl.BlockSpec((RET_HEADS, c, c), lambda b, t: (0, 0, 0)),
            pl.BlockSpec((c, RET_QK), lambda b, t: (0, 0)),
            pl.BlockSpec((c, RET_QK), lambda b, t: (0, 0)),
            pl.BlockSpec((1, RET_V), lambda b, t: (0, 0)),
        ],
        out_specs=pl.BlockSpec((rows, RET_V), lambda b, t: (row(b, t), 0)),
        out_shape=jax.ShapeDtypeStruct((batch * seq, RET_V), BF16),
        scratch_shapes=[pltpu.VMEM((RET_HEADS, RET_DK, RET_DV), F32)],
        compiler_params=_params("parallel", "arbitrary"),
        name="retention",
    )(proj, proj, proj, proj, cos_t, sin_t, dmask, qdec, kdec, ret_gn)


def _gdn_prepare_pieces(slot, r0, gq_ref, gk_ref, gv_ref, gab_ref, cw_ref, shift_ref, alog_ref, dtb_ref, buf_ref,
                        qkv_ref, gate_ref):
    c = GDN_CHUNK

    def column_block(j):
        width = 2 * LANES
        src_ref = (gq_ref, gk_ref, gv_ref)[j * width // GDN_QK]
        src_sl = slice(j * width % GDN_QK, j * width % GDN_QK + width)
        sl = slice(j * width, (j + 1) * width)
        buf_ref[BF16_TILE:BF16_TILE + c, sl] = src_ref[r0:r0 + c, src_sl]
        shifted = _dot(shift_ref[...], buf_ref[:, sl])
        buf_ref[0:BF16_TILE, sl] = buf_ref[c:c + BF16_TILE, sl]
        y = shifted[0:c] * cw_ref[CONV_WIDTH - 1:CONV_WIDTH, sl]
        for s in range(1, CONV_WIDTH):
            y = y + shifted[s * c:(s + 1) * c] * cw_ref[CONV_WIDTH - 1 - s:CONV_WIDTH - s, sl]
        y = _silu(y)
        for half in range(2):
            hsl = slice(half * LANES, (half + 1) * LANES)
            yh = y[:, hsl]
            if sl.start < 2 * GDN_QK:
                scale = GDN_DK ** -0.5 if sl.start < GDN_QK else 1.0
                yh = yh * (lax.rsqrt(jnp.sum(yh * yh, axis=-1, keepdims=True) + EPS) * scale)
            qkv_ref[slot, :, sl.start + half * LANES:sl.start + (half + 1) * LANES] = yh

    def gates():
        gab = gab_ref[r0:r0 + c, :]
        z = gab + dtb_ref[...]
        softplus = jnp.maximum(z, 0.0) + jnp.log(1.0 + jnp.exp(-jnp.abs(z)))
        gc = -jnp.exp(alog_ref[...]) * softplus
        row = lax.broadcasted_iota(jnp.int32, (c, LANES), 0)
        for s in (1, 2, 4, 8, 16, 32):
            gc = gc + jnp.where(row >= s, pltpu.roll(gc, s, 0), 0.0)
        gate_ref[slot, 0] = gc
        gate_ref[slot, 1] = jax.nn.sigmoid(gab)

    return [functools.partial(column_block, j) for j in range(GDN_CONV_DIM // (2 * LANES))] + [gates]


def _gdn_delta_rule_stages(slot, r0, gz_ref, gn_ref, o_ref, state_ref, qkv_ref, gate_ref):
    c = GDN_CHUNK
    gc = gate_ref[slot, 0]
    beta = gate_ref[slot, 1]
    gc_t = gc.T
    ii = lax.broadcasted_iota(jnp.int32, (c, c), 0)
    jj = lax.broadcasted_iota(jnp.int32, (c, c), 1)
    incl = ii >= jj
    strict = ii > jj
    gn = gn_ref[...]
    hs = range(GDN_HEADS)
    sls = [slice(hd * GDN_DK, (hd + 1) * GDN_DK) for hd in hs]
    q_l = [qkv_ref[slot, :, sls[hd]] for hd in hs]
    k_l = [qkv_ref[slot, :, GDN_QK + hd * GDN_DK:GDN_QK + (hd + 1) * GDN_DK] for hd in hs]
    v_l = [qkv_ref[slot, :, 2 * GDN_QK + hd * GDN_DV:2 * GDN_QK + (hd + 1) * GDN_DV] for hd in hs]
    gcol_l = [gc[:, hd:hd + 1] for hd in hs]
    glast_l = [gc[c - 1:c, hd:hd + 1] for hd in hs]
    bcol_l = [beta[:, GDN_HEADS + hd:GDN_HEADS + hd + 1] for hd in hs]
    m_l, x_l, qk_l, ecol_l = [], [], [], []
    for hd in hs:
        decay = jnp.where(incl, jnp.exp(jnp.where(incl, gcol_l[hd] - gc_t[hd:hd + 1, :], 0.0)), 0.0)
        kb = k_l[hd] * bcol_l[hd]
        both = _dot_nt(jnp.concatenate([kb, q_l[hd]], axis=0).astype(BF16), k_l[hd].astype(BF16))
        m_l.append(-(both[0:c] * jnp.where(strict, decay, 0.0)))
        qk_l.append(both[c:2 * c] * decay)
        e_col = jnp.exp(gcol_l[hd])
        ecol_l.append(e_col)
        x_l.append(jnp.concatenate([v_l[hd] * bcol_l[hd], kb * e_col], axis=1))
    yield
    eye = (ii == jj).astype(F32)
    t_l = [eye + m for m in m_l]
    mb_l = [m.astype(BF16) for m in m_l]
    m_l = [_dot(mb_l[hd], mb_l[hd]) for hd in hs]
    for it in range(1, 6):
        mb_l = [m.astype(BF16) for m in m_l]
        if it < 5:
            prod = [_dot(jnp.concatenate([m_l[hd], t_l[hd]], axis=0).astype(BF16), mb_l[hd]) for hd in hs]
            m_next = [pr[0:c] for pr in prod]
            t_l = [t_l[hd] + prod[hd][c:2 * c] for hd in hs]
            m_l = m_next
        else:
            t_l = [t_l[hd] + _dot(t_l[hd].astype(BF16), mb_l[hd]) for hd in hs]
        yield
    x_l = [_dot(t_l[hd].astype(BF16), x_l[hd].astype(BF16)) for hd in hs]
    yield
    st_l = [state_ref[hd] for hd in hs]
    ws_l = [_dot(jnp.concatenate([x_l[hd][:, GDN_DV:], q_l[hd] * ecol_l[hd]], axis=0).astype(BF16),
                 st_l[hd].astype(BF16)) for hd in hs]
    vn_l = [(x_l[hd][:, 0:GDN_DV] - ws_l[hd][0:c]).astype(BF16) for hd in hs]
    yield
    o_l = [ws_l[hd][c:2 * c] + _dot(qk_l[hd].astype(BF16), vn_l[hd]) for hd in hs]
    for hd in hs:
        k_dec_t = (k_l[hd] * jnp.exp(glast_l[hd] - gcol_l[hd])).T.astype(BF16)
        state_ref[hd] = st_l[hd] * jnp.exp(glast_l[hd]) + _dot(k_dec_t, vn_l[hd])
    yield
    for hd in hs:
        o = o_l[hd]
        o = o * lax.rsqrt(jnp.mean(o * o, axis=-1, keepdims=True) + EPS) * gn
        o_ref[r0:r0 + c, sls[hd]] = (o * _silu(gz_ref[r0:r0 + c, sls[hd]].astype(F32))).astype(BF16)


def _gdn_kernel(gq_ref, gk_ref, gv_ref, gz_ref, gab_ref, cw_ref, shift_ref, alog_ref, dtb_ref, gn_ref,
                o_ref, state_ref, buf_ref, qkv_ref, gate_ref):
    @pl.when(pl.program_id(1) == 0)
    def _():
        state_ref[...] = jnp.zeros_like(state_ref)
        buf_ref[0:BF16_TILE, :] = jnp.zeros((BF16_TILE, GDN_CONV_DIM), BF16)
        qkv_ref[...] = jnp.zeros_like(qkv_ref)
        gate_ref[...] = jnp.zeros_like(gate_ref)

    per_stage = 2
    for sub in range(GDN_STEP_CHUNKS):
        r0 = sub * GDN_CHUNK
        pieces = _gdn_prepare_pieces(sub, r0, gq_ref, gk_ref, gv_ref, gab_ref, cw_ref, shift_ref, alog_ref,
                                     dtb_ref, buf_ref, qkv_ref, gate_ref)
        for _ in _gdn_delta_rule_stages(sub, r0, gz_ref, gn_ref, o_ref, state_ref, qkv_ref, gate_ref):
            for piece in pieces[:per_stage]:
                piece()
            pieces = pieces[per_stage:]
        for piece in pieces:
            piece()


def _gdn(proj, gab, conv_w, a_log, dt_bias, gdn_gn, batch, seq):
    c = GDN_CHUNK
    rows = GDN_STEP_CHUNKS * c
    nb = seq // rows
    row = lambda b, t: b * nb + jnp.minimum(t, nb - 1)
    prev = lambda b, t: b * nb + jnp.maximum(t - 1, 0)
    pad = lambda a: jnp.pad(a.reshape(1, -1), ((0, 0), (0, LANES - a.shape[-1])))
    shift = np.zeros((CONV_WIDTH * c, BF16_TILE + c), np.float32)
    for s in range(CONV_WIDTH):
        shift[s * c + np.arange(c), BF16_TILE + np.arange(c) - s] = 1.0
    return pl.pallas_call(
        _gdn_kernel,
        grid=(batch, nb + 1),
        in_specs=[
            pl.BlockSpec((rows, GDN_QK), lambda b, t: (row(b, t), _BLK_GQ)),
            pl.BlockSpec((rows, GDN_QK), lambda b, t: (row(b, t), _BLK_GK)),
            pl.BlockSpec((rows, GDN_V), lambda b, t: (row(b, t), _BLK_GV)),
            pl.BlockSpec((rows, GDN_V), lambda b, t: (prev(b, t), _BLK_GZ)),
            pl.BlockSpec((rows, LANES), lambda b, t: (row(b, t), 0)),
            pl.BlockSpec((CONV_WIDTH, GDN_CONV_DIM), lambda b, t: (0, 0)),
            pl.BlockSpec((CONV_WIDTH * c, BF16_TILE + c), lambda b, t: (0, 0)),
            pl.BlockSpec((1, LANES), lambda b, t: (0, 0)),
            pl.BlockSpec((1, LANES), lambda b, t: (0, 0)),
            pl.BlockSpec((1, GDN_DV), lambda b, t: (0, 0)),
        ],
        out_specs=pl.BlockSpec((rows, GDN_V), lambda b, t: (prev(b, t), 0)),
        out_shape=jax.ShapeDtypeStruct((batch * seq, GDN_V), BF16),
        scratch_shapes=[
            pltpu.VMEM((GDN_HEADS, GDN_DK, GDN_DV), F32),
            pltpu.VMEM((BF16_TILE + c, GDN_CONV_DIM), BF16),
            pltpu.VMEM((GDN_STEP_CHUNKS, c, GDN_CONV_DIM), F32),
            pltpu.VMEM((GDN_STEP_CHUNKS, 2, c, LANES), F32),
        ],
        compiler_params=_params("parallel", "arbitrary"),
        name="gdn",
    )(proj, proj, proj, proj, gab, conv_w, jnp.asarray(shift, BF16), pad(a_log), pad(dt_bias), gdn_gn.reshape(1, -1))


def _mix_out_kernel(h_ref, oret_ref, ogdn_ref, mgr_ref, mgg_ref, wbr_ref, wbg_ref, wo_ref, out_ref):
    y_ret = _dot(oret_ref[...], wbr_ref[...])
    y_gdn = _dot(ogdn_ref[...], wbg_ref[...])
    merged = (jax.nn.sigmoid(mgr_ref[...].astype(F32)) * y_ret
              + jax.nn.sigmoid(mgg_ref[...].astype(F32)) * y_gdn)
    out_ref[...] = h_ref[...] + _dot(merged.astype(BF16), wo_ref[...])


def _mix_out(h, o_ret, o_gdn, proj, w_br_ret, w_br_gdn, w_out, tm=512):
    n = h.shape[0]
    tile = lambda blk: pl.BlockSpec((tm, D_MODEL), lambda i: (i, blk))
    weight = pl.BlockSpec((D_MODEL, D_MODEL), lambda i: (0, 0))
    return pl.pallas_call(
        _mix_out_kernel,
        grid=(n // tm,),
        in_specs=[tile(0), tile(0), tile(0), tile(_BLK_MGR), tile(_BLK_MGG), weight, weight, weight],
        out_specs=tile(0),
        out_shape=jax.ShapeDtypeStruct((n, D_MODEL), F32),
        compiler_params=_params("parallel"),
        name="mix_out",
    )(h, o_ret, o_gdn, proj, proj, w_br_ret, w_br_gdn, w_out)


def _ple_tail(h1, p_ref, gple_ref, wple_ref, wpg_ref, gfin_ref, out_ref, final):
    ple = _dot(p_ref[0].astype(BF16), wple_ref[...])
    gate = jax.nn.sigmoid(_dot(_rms(h1, gple_ref[...]).astype(BF16), wpg_ref[...]))
    h2 = h1 + gate * ple
    out_ref[...] = _rms(h2, gfin_ref[...]) if final else h2


def _ffn_kernel(h_ref, g_ref, wg_ref, wu_ref, wd_ref, p_ref, gple_ref, wple_ref, wpg_ref, gfin_ref,
                out_ref, *, final, tf):
    h = h_ref[...]
    u = _rms(h, g_ref[...]).astype(BF16)
    for j in range(D_FF // tf):
        cols = slice(j * tf, (j + 1) * tf)
        act = _silu(_dot(u, wg_ref[:, cols])) * _dot(u, wu_ref[:, cols])
        h = h + _dot(act.astype(BF16), wd_ref[cols, :])
    _ple_tail(h, p_ref, gple_ref, wple_ref, wpg_ref, gfin_ref, out_ref, final)


def _ffn(h, g, wg, wu, wd, p, layer, g_ple, w_ple, w_ple_gate, g_final, final, tm=512, tf=256):
    n = h.shape[0]
    return pl.pallas_call(
        functools.partial(_ffn_kernel, final=final, tf=tf),
        grid=(n // tm,),
        in_specs=[
            pl.BlockSpec((tm, D_MODEL), lambda i: (i, 0)),
            _resident((1, D_MODEL)),
            _resident((D_MODEL, D_FF)),
            _resident((D_MODEL, D_FF)),
            _resident((D_FF, D_MODEL)),
            pl.BlockSpec((1, tm, PLE_DIM), lambda i: (layer, i, 0)),
            _resident((1, D_MODEL)),
            _resident((PLE_DIM, D_MODEL)),
            _resident((D_MODEL, D_MODEL)),
            _resident((1, D_MODEL)),
        ],
        out_specs=pl.BlockSpec((tm, D_MODEL), lambda i: (i, 0)),
        out_shape=jax.ShapeDtypeStruct((n, D_MODEL), F32),
        compiler_params=_params("parallel"),
        name="ffn_ple",
    )(h, g, wg, wu, wd, p, g_ple, w_ple, w_ple_gate, g_final)


def _router_kernel(h_ref, g_ref, wr_ref, route_ref):
    u = _rms(h_ref[...], g_ref[...])
    w = wr_ref[...]
    u_hi, w_hi = u.astype(BF16), w.astype(BF16)
    u_lo = (u - u_hi.astype(F32)).astype(BF16)
    w_lo = (w - w_hi.astype(F32)).astype(BF16)
    logits = _dot(u_hi, w_hi) + (_dot(u_lo, w_hi) + _dot(u_hi, w_lo))
    lane = lax.broadcasted_iota(jnp.int32, logits.shape, 1)
    logits = jnp.where(lane < N_EXPERTS, logits, -jnp.inf)
    m1 = jnp.max(logits, axis=-1, keepdims=True)
    i1 = jnp.min(jnp.where(logits == m1, lane, LANES), axis=-1, keepdims=True)
    rest = jnp.where(lane == i1, -jnp.inf, logits)
    m2 = jnp.max(rest, axis=-1, keepdims=True)
    i2 = jnp.min(jnp.where(rest == m2, lane, LANES), axis=-1, keepdims=True)
    e2 = jnp.exp(m2 - m1)
    w1 = 1.0 / (1.0 + e2)
    route_ref[...] = (jnp.where(lane == 0, i1.astype(F32), 0.0) + jnp.where(lane == 1, i2.astype(F32), 0.0)
                      + jnp.where(lane == 2, w1, 0.0) + jnp.where(lane == 3, e2 * w1, 0.0))


def _router(h, g, w_router, tm=1024):
    n = h.shape[0]
    return pl.pallas_call(
        _router_kernel,
        grid=(n // tm,),
        in_specs=[
            pl.BlockSpec((tm, D_MODEL), lambda i: (i, 0)),
            pl.BlockSpec((1, D_MODEL), lambda i: (0, 0)),
            pl.BlockSpec((D_MODEL, LANES), lambda i: (0, 0)),
        ],
        out_specs=pl.BlockSpec((tm, LANES), lambda i: (i, 0)),
        out_shape=jax.ShapeDtypeStruct((n, LANES), F32),
        compiler_params=_params("parallel"),
        name="router",
    )(h, g, w_router)


def _routing_tables(route, n):
    n_pairs = 2 * n
    fill_bit = 1 << 15
    assert n_pairs <= fill_bit and MOE_TILE <= fill_bit
    expert = route[:, 0:2].astype(jnp.int32).reshape(-1)
    ids = jnp.arange(N_EXPERTS, dtype=jnp.int32)
    counts = jnp.sum((expert[:, None] == ids[None, :]).astype(jnp.int32), axis=0)
    padded = ((counts + MOE_TILE - 1) // MOE_TILE) * MOE_TILE
    keys = (expert << 16) | jnp.arange(n_pairs, dtype=jnp.int32)
    fill_idx = jnp.arange(MOE_TILE, dtype=jnp.int32)[None, :]
    fill_keys = jnp.where(fill_idx < (padded - counts)[:, None], (ids[:, None] << 16) | fill_bit | fill_idx,
                          jnp.iinfo(jnp.int32).max)
    keys = jnp.sort(jnp.concatenate([keys, fill_keys.reshape(-1)]))
    lookahead = jnp.full(((MOE_STAGES - 1) * MOE_TILE,), jnp.iinfo(jnp.int32).max, jnp.int32)
    keys = jnp.concatenate([keys, lookahead])
    low = keys & (2 * fill_bit - 1)
    valid = low < fill_bit
    row_expert = jnp.minimum(keys >> 16, N_EXPERTS - 1)
    spare = n_pairs + row_expert * MOE_TILE + jnp.minimum(low - fill_bit, MOE_TILE - 1)
    src_row = jnp.where(valid, low >> 1, 0)
    dst_row = jnp.where(valid, (low & 1) * n + (low >> 1), spare)
    tile_expert = row_expert[::MOE_TILE]
    n_used = (jnp.sum(padded) // MOE_TILE).reshape(1)
    return src_row, dst_row, tile_expert, n_used


def _rows_start(count, src_of, dst_of, sem, unrolled):
    def one(r):
        pltpu.make_async_copy(src_of(r), dst_of(r), sem).start()
    if unrolled:
        for r in range(count):
            one(r)
    else:
        def body(r, carry):
            one(r)
            return carry
        lax.fori_loop(0, count, body, 0, unroll=8)


def _moe_gemm_kernel(src_row_ref, dst_row_ref, tile_expert_ref, n_used_ref, h_hbm, g_ref, wg_ref, wu_ref,
                     wd_ref, y_hbm, xs_ref, ys_ref, gsem, ssem):
    del tile_expert_ref
    i = pl.program_id(0)
    last = n_used_ref[0] - 1
    rows = pl.ds(0, MOE_TILE)
    slot = lambda tile: tile % MOE_STAGES

    def gather_start(tile, unrolled):
        _rows_start(MOE_TILE, lambda r: h_hbm.at[pl.ds(src_row_ref[tile * MOE_TILE + r], 1)],
                    lambda r: xs_ref.at[slot(tile), pl.ds(r, 1)], gsem.at[slot(tile)], unrolled)

    def gather_wait(tile):
        pltpu.make_async_copy(h_hbm.at[rows], xs_ref.at[slot(tile)], gsem.at[slot(tile)]).wait()

    def scatter_start(tile, unrolled):
        _rows_start(MOE_TILE, lambda r: ys_ref.at[slot(tile), pl.ds(r, 1)],
                    lambda r: y_hbm.at[pl.ds(dst_row_ref[tile * MOE_TILE + r], 1)], ssem.at[slot(tile)], unrolled)

    def scatter_wait(tile):
        pltpu.make_async_copy(ys_ref.at[slot(tile)], y_hbm.at[rows], ssem.at[slot(tile)]).wait()

    def compute(tile):
        u = _rms(xs_ref[slot(tile)], g_ref[...]).astype(BF16)
        act = _silu(_dot(u, wg_ref[0])) * _dot(u, wu_ref[0])
        ys_ref[slot(tile)] = _dot(act.astype(BF16), wd_ref[0])

    @pl.when(i == 0)
    def _():
        for tile in range(MOE_STAGES):
            gather_start(tile, False)
        ys_ref[1] = jnp.zeros((MOE_TILE, D_MODEL), F32)
        n_pairs = y_hbm.shape[0] - N_EXPERTS * MOE_TILE
        for e in range(N_EXPERTS):
            block = pltpu.make_async_copy(ys_ref.at[1], y_hbm.at[pl.ds(n_pairs + e * MOE_TILE, MOE_TILE)],
                                          ssem.at[1])
            block.start()
            block.wait()
        gather_wait(0)
        compute(0)

    @pl.when(jnp.logical_and(i >= 1, i <= last))
    def _():
        gather_wait(i)
        gather_start(i + MOE_STAGES - 1, True)
        scatter_start(i - 1, True)
        compute(i)

    @pl.when(jnp.logical_and(i >= 2, i <= last))
    def _():
        scatter_wait(i - 2)

    @pl.when(i == last)
    def _():
        for ahead in range(1, MOE_STAGES):
            gather_wait(i + ahead)

        @pl.when(i >= 1)
        def _():
            scatter_wait(i - 1)

        scatter_start(i, False)
        scatter_wait(i)


def _moe_gemm(h, g, src_row, dst_row, tile_expert, n_used, wg, wu, wd):
    n = h.shape[0]
    n_tiles = src_row.shape[0] // MOE_TILE - (MOE_STAGES - 1)
    expert_block = lambda i, sr, dr, te, nu: (te[jnp.minimum(i, nu[0] - 1)], 0, 0)
    return pl.pallas_call(
        _moe_gemm_kernel,
        grid_spec=pltpu.PrefetchScalarGridSpec(
            num_scalar_prefetch=4,
            grid=(n_tiles,),
            in_specs=[
                pl.BlockSpec(memory_space=pl.ANY),
                pl.BlockSpec((1, D_MODEL), lambda i, sr, dr, te, nu: (0, 0)),
                pl.BlockSpec((1, D_MODEL, D_FF_EXPERT), expert_block),
                pl.BlockSpec((1, D_MODEL, D_FF_EXPERT), expert_block),
                pl.BlockSpec((1, D_FF_EXPERT, D_MODEL), expert_block),
            ],
            out_specs=pl.BlockSpec(memory_space=pl.ANY),
            scratch_shapes=[
                pltpu.VMEM((MOE_STAGES, MOE_TILE, D_MODEL), F32),
                pltpu.VMEM((MOE_STAGES, MOE_TILE, D_MODEL), F32),
                pltpu.SemaphoreType.DMA((MOE_STAGES,)),
                pltpu.SemaphoreType.DMA((MOE_STAGES,)),
            ],
        ),
        out_shape=jax.ShapeDtypeStruct((2 * n + N_EXPERTS * MOE_TILE, D_MODEL), F32),
        compiler_params=_params("arbitrary"),
        name="moe_gemm",
    )(src_row, dst_row, tile_expert, n_used, h, g, wg, wu, wd)


def _moe_tail_kernel(h_ref, route_ref, y0_ref, y1_ref, p_ref, gple_ref, wple_ref, wpg_ref, gfin_ref, out_ref,
                     *, final):
    route = route_ref[...]
    h1 = h_ref[...] + route[:, 2:3] * y0_ref[...] + route[:, 3:4] * y1_ref[...]
    _ple_tail(h1, p_ref, gple_ref, wple_ref, wpg_ref, gfin_ref, out_ref, final)


def _moe_tail(h, route, y, p, layer, g_ple, w_ple, w_ple_gate, g_final, final, tm=512):
    n = h.shape[0]
    return pl.pallas_call(
        functools.partial(_moe_tail_kernel, final=final),
        grid=(n // tm,),
        in_specs=[
            pl.BlockSpec((tm, D_MODEL), lambda i: (i, 0)),
            pl.BlockSpec((tm, LANES), lambda i: (i, 0)),
            pl.BlockSpec((tm, D_MODEL), lambda i: (i, 0)),
            pl.BlockSpec((tm, D_MODEL), lambda i: (n // tm + i, 0)),
            pl.BlockSpec((1, tm, PLE_DIM), lambda i: (layer, i, 0)),
            _resident((1, D_MODEL)),
            _resident((PLE_DIM, D_MODEL)),
            _resident((D_MODEL, D_MODEL)),
            _resident((1, D_MODEL)),
        ],
        out_specs=pl.BlockSpec((tm, D_MODEL), lambda i: (i, 0)),
        out_shape=jax.ShapeDtypeStruct((n, D_MODEL), F32),
        compiler_params=_params("parallel"),
        name="moe_tail",
    )(h, route, y, y, p, g_ple, w_ple, w_ple_gate, g_final)


def _moe(h, g, w_router, wg, wu, wd, p, layer, g_ple, w_ple, w_ple_gate, g_final, final):
    n = h.shape[0]
    route = _router(h, g, w_router)
    src_row, dst_row, tile_expert, n_used = _routing_tables(route, n)
    y = _moe_gemm(h, g, src_row, dst_row, tile_expert, n_used, wg, wu, wd)
    return _moe_tail(h, route, y, p, layer, g_ple, w_ple, w_ple_gate, g_final, final)


def _rotary_tables(seq):
    half = RET_DK // 2
    pos = jnp.arange(seq, dtype=F32)
    inv_freq = ROPE_THETA ** (-jnp.arange(half, dtype=F32) / half)
    ang = pos[:, None] * inv_freq[None, :]
    cos, sin = jnp.cos(ang), jnp.sin(ang)
    return jnp.concatenate([cos, cos], axis=-1), jnp.concatenate([-sin, sin], axis=-1)


def kernel(x, p, w_in, conv_w, a_log, dt_bias, ret_gn, gdn_gn, w_br_ret, w_br_gdn, w_out, norm_mix,
           norm_ffn, norm_ple, w_ple, w_ple_gate, ffn_w_gate, ffn_w_up, ffn_w_down, router,
           exp_w_gate, exp_w_up, exp_w_down, norm_final):
    batch, seq, _ = x.shape
    depth = w_in.shape[0]
    n = batch * seq
    cos_t, sin_t = _rotary_tables(seq)
    g_final = norm_final.reshape(1, -1)
    h = x.reshape(n, D_MODEL)
    for i in range(depth):
        w_gab = jnp.pad(w_in[i, :, OFF_GA:OFF_MG], ((0, 0), (0, LANES - 2 * GDN_HEADS))).astype(BF16)
        proj, gab = _in_proj(h, norm_mix[i].reshape(1, -1), w_in[i, :, :OFF_GA].astype(BF16),
                             w_in[i, :, OFF_MG:].astype(BF16), w_gab)
        o_ret = _retention(proj, cos_t, sin_t, ret_gn[i].reshape(1, -1), batch, seq)
        o_gdn = _gdn(proj, gab, conv_w[i], a_log[i], dt_bias[i], gdn_gn[i], batch, seq)
        h = _mix_out(h, o_ret, o_gdn, proj, w_br_ret[i].astype(BF16), w_br_gdn[i].astype(BF16),
                     w_out[i].astype(BF16))
        final = i == depth - 1
        tail = (p.reshape(depth, n, PLE_DIM), i, norm_ple[i].reshape(1, -1), w_ple[i].astype(BF16),
                w_ple_gate[i].astype(BF16), g_final, final)
        j = i // 2
        g_ffn = norm_ffn[i].reshape(1, -1)
        if i % 2 == 0:
            h = _ffn(h, g_ffn, ffn_w_gate[j].astype(BF16), ffn_w_up[j].astype(BF16),
                     ffn_w_down[j].astype(BF16), *tail)
        else:
            w_router = jnp.pad(router[j], ((0, 0), (0, LANES - N_EXPERTS)))
            h = _moe(h, g_ffn, w_router, exp_w_gate[j].astype(BF16), exp_w_up[j].astype(BF16),
                     exp_w_down[j].astype(BF16), *tail)
    return h.reshape(batch, seq, D_MODEL)
```

```python
import functools

import numpy as np
import jax
import jax.numpy as jnp
from jax import lax
from jax.experimental import pallas as pl
from jax.experimental.pallas import tpu as pltpu

F32 = jnp.float32
BF16 = jnp.bfloat16

D_MODEL = 1024
PLE_DIM = 256
EPS = 1e-6
ROPE_THETA = 10000.0
RET_HEADS = 4
RET_DK = 128
RET_DV = 256
RET_QK = RET_HEADS * RET_DK
RET_V = RET_HEADS * RET_DV
GDN_HEADS = 8
GDN_DK = 128
GDN_DV = 128
GDN_QK = GDN_HEADS * GDN_DK
GDN_V = GDN_HEADS * GDN_DV
GDN_CONV_DIM = 2 * GDN_QK + GDN_V
CONV_WIDTH = 4
D_FF = 2816
N_EXPERTS = 8
D_FF_EXPERT = 1408
OFF_GA = 2 * RET_QK + 2 * RET_V + GDN_CONV_DIM + GDN_V
OFF_MG = OFF_GA + 2 * GDN_HEADS
N_MAIN = OFF_GA + 2 * D_MODEL
LANES = 128
GDN_CHUNK = 64
GDN_STEP_CHUNKS = 4
RET_CHUNK = 256
RET_STEP_CHUNKS = 2
MOE_TILE = 512
MOE_STAGES = 3
BF16_TILE = 16
VMEM_LIMIT = 56 * 1024 * 1024

_BLK_RV, _BLK_RG, _BLK_GQ, _BLK_GK, _BLK_GV, _BLK_GZ, _BLK_MGR, _BLK_MGG = 1, 2, 3, 4, 5, 6, 7, 8


def _silu(x):
    return x * jax.nn.sigmoid(x)


def _rms(x, g):
    return x * lax.rsqrt(jnp.mean(x * x, axis=-1, keepdims=True) + EPS) * g


def _dot(a, b):
    return jnp.dot(a, b, preferred_element_type=F32)


def _dot_nt(a, b):
    return lax.dot_general(a, b, (((1,), (1,)), ((), ())), preferred_element_type=F32)


def _params(*sem):
    return pltpu.CompilerParams(dimension_semantics=sem, vmem_limit_bytes=VMEM_LIMIT)


def _in_proj_kernel(x_ref, g_ref, wa_ref, wb_ref, wg_ref, om_ref, og_ref, *, tn):
    u = _rms(x_ref[...], g_ref[...]).astype(BF16)
    for j in range(N_MAIN // tn):
        w_ref, k = (wa_ref, j) if j < OFF_GA // tn else (wb_ref, j - OFF_GA // tn)
        om_ref[:, j * tn:(j + 1) * tn] = _dot(u, w_ref[:, k * tn:(k + 1) * tn]).astype(BF16)
    og_ref[...] = _dot(u, wg_ref[...])


def _resident(shape):
    return pl.BlockSpec(shape, lambda *_: (0,) * len(shape), pipeline_mode=pl.Buffered(1))


def _in_proj(h, g, w_a, w_b, w_gab, tm=512, tn=1024):
    n = h.shape[0]
    return pl.pallas_call(
        functools.partial(_in_proj_kernel, tn=tn),
        grid=(n // tm,),
        in_specs=[
            pl.BlockSpec((tm, D_MODEL), lambda i: (i, 0)),
            _resident((1, D_MODEL)),
            _resident((D_MODEL, OFF_GA)),
            _resident((D_MODEL, N_MAIN - OFF_GA)),
            _resident((D_MODEL, LANES)),
        ],
        out_specs=[
            pl.BlockSpec((tm, N_MAIN), lambda i: (i, 0)),
            pl.BlockSpec((tm, LANES), lambda i: (i, 0)),
        ],
        out_shape=[
            jax.ShapeDtypeStruct((n, N_MAIN), BF16),
            jax.ShapeDtypeStruct((n, LANES), F32),
        ],
        compiler_params=_params("parallel"),
        name="in_proj",
    )(h, g, w_a, w_b, w_gab)


def _ret_kernel(q_ref, k_ref, v_ref, rg_ref, cos_ref, sin_ref, dmask_ref, qdec_ref, kdec_ref,
                gn_ref, o_ref, state_ref, *, chunk_decay):
    @pl.when(pl.program_id(1) == 0)
    def _():
        state_ref[...] = jnp.zeros_like(state_ref)

    for sub in range(RET_STEP_CHUNKS):
        rs = slice(sub * RET_CHUNK, (sub + 1) * RET_CHUNK)
        cos = cos_ref[rs, :]
        sin = sin_ref[rs, :]
        for hd in range(RET_HEADS):
            qk_sl = slice(hd * RET_DK, (hd + 1) * RET_DK)
            v_sl = slice(hd * RET_DV, (hd + 1) * RET_DV)
            q = q_ref[rs, qk_sl].astype(F32)
            k = k_ref[rs, qk_sl].astype(F32)
            q = q * cos + pltpu.roll(q, RET_DK // 2, 1) * sin
            k = (k * cos + pltpu.roll(k, RET_DK // 2, 1) * sin) * (RET_DK ** -0.5)
            v = v_ref[rs, v_sl]
            scores = _dot_nt(q.astype(BF16), k.astype(BF16)) * dmask_ref[hd]
            state = state_ref[hd]
            o = _dot(scores.astype(BF16), v) + _dot((q * qdec_ref[:, qk_sl]).astype(BF16), state.astype(BF16))
            k_dec_t = (k * kdec_ref[:, qk_sl]).T.astype(BF16)
            state_ref[hd] = state * chunk_decay[hd] + _dot(k_dec_t, v)
            mu = jnp.mean(o, axis=-1, keepdims=True)
            oc = o - mu
            var = jnp.mean(oc * oc, axis=-1, keepdims=True)
            o = oc * lax.rsqrt(var + EPS) * gn_ref[:, v_sl]
            o_ref[rs, v_sl] = (o * _silu(rg_ref[rs, v_sl].astype(F32))).astype(BF16)


def _retention(proj, cos_t, sin_t, ret_gn, batch, seq):
    c = RET_CHUNK
    rows = RET_STEP_CHUNKS * c
    nc = seq // rows
    log_gamma = np.log1p(-np.exp2(-5.0 - np.arange(RET_HEADS, dtype=np.float64)))
    chunk_decay = tuple(float(np.exp(c * lg)) for lg in log_gamma)
    lg = jnp.asarray(log_gamma, F32)
    idx = jnp.arange(c, dtype=F32)
    rel = idx[:, None] - idx[None, :]
    causal = rel >= 0
    dmask = jnp.where(causal[None], jnp.exp(jnp.where(causal, rel, 0.0)[None] * lg[:, None, None]), 0.0)
    qdec = jnp.repeat(jnp.exp((idx + 1.0)[:, None] * lg[None, :]), RET_DK, axis=1)
    kdec = jnp.repeat(jnp.exp((c - 1.0 - idx)[:, None] * lg[None, :]), RET_DK, axis=1)
    row = lambda b, t: b * nc + t
    return pl.pallas_call(
        functools.partial(_ret_kernel, chunk_decay=chunk_decay),
        grid=(batch, nc),
        in_specs=[
            pl.BlockSpec((rows, RET_QK), lambda b, t: (row(b, t), 0)),
            pl.BlockSpec((rows, RET_QK), lambda b, t: (row(b, t), 1)),
            pl.BlockSpec((rows, RET_V), lambda b, t: (row(b, t), _BLK_RV)),
            pl.BlockSpec((rows, RET_V), lambda b, t: (row(b, t), _BLK_RG)),
            pl.BlockSpec((rows, RET_DK), lambda b, t: (t, 0)),
            pl.BlockSpec((rows, RET_DK), lambda b, t: (t, 0)),
            pl.BlockSpec((RET_HEADS, c, c), lambda b, t: (0, 0, 0)),
            pl.BlockSpec((c, RET_QK), lambda b, t: (0, 0)),
            pl.BlockSpec((c, RET_QK), lambda b, t: (0, 0)),
            pl.BlockSpec((1, RET_V), lambda b, t: (0, 0)),
        ],
        out_specs=pl.BlockSpec((rows, RET_V), lambda b, t: (row(b, t), 0)),
        out_shape=jax.ShapeDtypeStruct((batch * seq, RET_V), BF16),
        scratch_shapes=[pltpu.VMEM((RET_HEADS, RET_DK, RET_DV), F32)],
        compiler_params=_params("parallel", "arbitrary"),
        name="retention",
    )(proj, proj, proj, proj, cos_t, sin_t, dmask, qdec, kdec, ret_gn)


def _gdn_prepare_pieces(slot, r0, gq_ref, gk_ref, gv_ref, gab_ref, cw_ref, shift_ref, alog_ref, dtb_ref, buf_ref,
                        qkv_ref, gate_ref):
    c = GDN_CHUNK

    def column_block(j):
        width = 2 * LANES
        src_ref = (gq_ref, gk_ref, gv_ref)[j * width // GDN_QK]
        src_sl = slice(j * width % GDN_QK, j * width % GDN_QK + width)
        sl = slice(j * width, (j + 1) * width)
        buf_ref[BF16_TILE:BF16_TILE + c, sl] = src_ref[r0:r0 + c, src_sl]
        shifted = _dot(shift_ref[...], buf_ref[:, sl])
        buf_ref[0:BF16_TILE, sl] = buf_ref[c:c + BF16_TILE, sl]
        y = shifted[0:c] * cw_ref[CONV_WIDTH - 1:CONV_WIDTH, sl]
        for s in range(1, CONV_WIDTH):
            y = y + shifted[s * c:(s + 1) * c] * cw_ref[CONV_WIDTH - 1 - s:CONV_WIDTH - s, sl]
        y = _silu(y)
        for half in range(2):
            hsl = slice(half * LANES, (half + 1) * LANES)
            yh = y[:, hsl]
            if sl.start < 2 * GDN_QK:
                scale = GDN_DK ** -0.5 if sl.start < GDN_QK else 1.0
                yh = yh * (lax.rsqrt(jnp.sum(yh * yh, axis=-1, keepdims=True) + EPS) * scale)
            qkv_ref[slot, :, sl.start + half * LANES:sl.start + (half + 1) * LANES] = yh

    def gates():
        gab = gab_ref[r0:r0 + c, :]
        z = gab + dtb_ref[...]
        softplus = jnp.maximum(z, 0.0) + jnp.log(1.0 + jnp.exp(-jnp.abs(z)))
        gc = -jnp.exp(alog_ref[...]) * softplus
        row = lax.broadcasted_iota(jnp.int32, (c, LANES), 0)
        for s in (1, 2, 4, 8, 16, 32):
            gc = gc + jnp.where(row >= s, pltpu.roll(gc, s, 0), 0.0)
        gate_ref[slot, 0] = gc
        gate_ref[slot, 1] = jax.nn.sigmoid(gab)

    return [functools.partial(column_block, j) for j in range(GDN_CONV_DIM // (2 * LANES))] + [gates]


def _gdn_delta_rule_stages(slot, r0, gz_ref, gn_ref, o_ref, state_ref, qkv_ref, gate_ref):
    c = GDN_CHUNK
    gc = gate_ref[slot, 0]
    beta = gate_ref[slot, 1]
    gc_t = gc.T
    ii = lax.broadcasted_iota(jnp.int32, (c, c), 0)
    jj = lax.broadcasted_iota(jnp.int32, (c, c), 1)
    incl = ii >= jj
    strict = ii > jj
    gn = gn_ref[...]
    hs = range(GDN_HEADS)
    sls = [slice(hd * GDN_DK, (hd + 1) * GDN_DK) for hd in hs]
    q_l = [qkv_ref[slot, :, sls[hd]] for hd in hs]
    k_l = [qkv_ref[slot, :, GDN_QK + hd * GDN_DK:GDN_QK + (hd + 1) * GDN_DK] for hd in hs]
    v_l = [qkv_ref[slot, :, 2 * GDN_QK + hd * GDN_DV:2 * GDN_QK + (hd + 1) * GDN_DV] for hd in hs]
    gcol_l = [gc[:, hd:hd + 1] for hd in hs]
    glast_l = [gc[c - 1:c, hd:hd + 1] for hd in hs]
    bcol_l = [beta[:, GDN_HEADS + hd:GDN_HEADS + hd + 1] for hd in hs]
    m_l, x_l, qk_l, ecol_l = [], [], [], []
    for hd in hs:
        decay = jnp.where(incl, jnp.exp(jnp.where(incl, gcol_l[hd] - gc_t[hd:hd + 1, :], 0.0)), 0.0)
        kb = k_l[hd] * bcol_l[hd]
        both = _dot_nt(jnp.concatenate([kb, q_l[hd]], axis=0).astype(BF16), k_l[hd].astype(BF16))
        m_l.append(-(both[0:c] * jnp.where(strict, decay, 0.0)))
        qk_l.append(both[c:2 * c] * decay)
        e_col = jnp.exp(gcol_l[hd])
        ecol_l.append(e_col)
        x_l.append(jnp.concatenate([v_l[hd] * bcol_l[hd], kb * e_col], axis=1))
    yield
    eye = (ii == jj).astype(F32)
    t_l = [eye + m for m in m_l]
    mb_l = [m.astype(BF16) for m in m_l]
    m_l = [_dot(mb_l[hd], mb_l[hd]) for hd in hs]
    for it in range(1, 6):
        mb_l = [m.astype(BF16) for m in m_l]
        if it < 5:
            prod = [_dot(jnp.concatenate([m_l[hd], t_l[hd]], axis=0).astype(BF16), mb_l[hd]) for hd in hs]
            m_next = [pr[0:c] for pr in prod]
            t_l = [t_l[hd] + prod[hd][c:2 * c] for hd in hs]
            m_l = m_next
        else:
            t_l = [t_l[hd] + _dot(t_l[hd].astype(BF16), mb_l[hd]) for hd in hs]
        yield
    x_l = [_dot(t_l[hd].astype(BF16), x_l[hd].astype(BF16)) for hd in hs]
    yield
    st_l = [state_ref[hd] for hd in hs]
    ws_l = [_dot(jnp.concatenate([x_l[hd][:, GDN_DV:], q_l[hd] * ecol_l[hd]], axis=0).astype(BF16),
                 st_l[hd].astype(BF16)) for hd in hs]
    vn_l = [(x_l[hd][:, 0:GDN_DV] - ws_l[hd][0:c]).astype(BF16) for hd in hs]
    yield
    o_l = [ws_l[hd][c:2 * c] + _dot(qk_l[hd].astype(BF16), vn_l[hd]) for hd in hs]
    for hd in hs:
        k_dec_t = (k_l[hd] * jnp.exp(glast_l[hd] - gcol_l[hd])).T.astype(BF16)
        state_ref[hd] = st_l[hd] * jnp.exp(glast_l[hd]) + _dot(k_dec_t, vn_l[hd])
    yield
    for hd in hs:
        o = o_l[hd]
        o = o * lax.rsqrt(jnp.mean(o * o, axis=-1, keepdims=True) + EPS) * gn
        o_ref[r0:r0 + c, sls[hd]] = (o * _silu(gz_ref[r0:r0 + c, sls[hd]].astype(F32))).astype(BF16)


def _gdn_kernel(gq_ref, gk_ref, gv_ref, gz_ref, gab_ref, cw_ref, shift_ref, alog_ref, dtb_ref, gn_ref,
                o_ref, state_ref, buf_ref, qkv_ref, gate_ref):
    @pl.when(pl.program_id(1) == 0)
    def _():
        state_ref[...] = jnp.zeros_like(state_ref)
        buf_ref[0:BF16_TILE, :] = jnp.zeros((BF16_TILE, GDN_CONV_DIM), BF16)
        qkv_ref[...] = jnp.zeros_like(qkv_ref)
        gate_ref[...] = jnp.zeros_like(gate_ref)

    per_stage = 2
    for sub in range(GDN_STEP_CHUNKS):
        r0 = sub * GDN_CHUNK
        pieces = _gdn_prepare_pieces(sub, r0, gq_ref, gk_ref, gv_ref, gab_ref, cw_ref, shift_ref, alog_ref,
                                     dtb_ref, buf_ref, qkv_ref, gate_ref)
        for _ in _gdn_delta_rule_stages(sub, r0, gz_ref, gn_ref, o_ref, state_ref, qkv_ref, gate_ref):
            for piece in pieces[:per_stage]:
                piece()
            pieces = pieces[per_stage:]
        for piece in pieces:
            piece()


def _gdn(proj, gab, conv_w, a_log, dt_bias, gdn_gn, batch, seq):
    c = GDN_CHUNK
    rows = GDN_STEP_CHUNKS * c
    nb = seq // rows
    row = lambda b, t: b * nb + jnp.minimum(t, nb - 1)
    prev = lambda b, t: b * nb + jnp.maximum(t - 1, 0)
    pad = lambda a: jnp.pad(a.reshape(1, -1), ((0, 0), (0, LANES - a.shape[-1])))
    shift = np.zeros((CONV_WIDTH * c, BF16_TILE + c), np.float32)
    for s in range(CONV_WIDTH):
        shift[s * c + np.arange(c), BF16_TILE + np.arange(c) - s] = 1.0
    return pl.pallas_call(
        _gdn_kernel,
        grid=(batch, nb + 1),
        in_specs=[
            pl.BlockSpec((rows, GDN_QK), lambda b, t: (row(b, t), _BLK_GQ)),
            pl.BlockSpec((rows, GDN_QK), lambda b, t: (row(b, t), _BLK_GK)),
            pl.BlockSpec((rows, GDN_V), lambda b, t: (row(b, t), _BLK_GV)),
            pl.BlockSpec((rows, GDN_V), lambda b, t: (prev(b, t), _BLK_GZ)),
            pl.BlockSpec((rows, LANES), lambda b, t: (row(b, t), 0)),
            pl.BlockSpec((CONV_WIDTH, GDN_CONV_DIM), lambda b, t: (0, 0)),
            pl.BlockSpec((CONV_WIDTH * c, BF16_TILE + c), lambda b, t: (0, 0)),
            pl.BlockSpec((1, LANES), lambda b, t: (0, 0)),
            pl.BlockSpec((1, LANES), lambda b, t: (0, 0)),
            pl.BlockSpec((1, GDN_DV), lambda b, t: (0, 0)),
        ],
        out_specs=pl.BlockSpec((rows, GDN_V), lambda b, t: (prev(b, t), 0)),
        out_shape=jax.ShapeDtypeStruct((batch * seq, GDN_V), BF16),
        scratch_shapes=[
            pltpu.VMEM((GDN_HEADS, GDN_DK, GDN_DV), F32),
            pltpu.VMEM((BF16_TILE + c, GDN_CONV_DIM), BF16),
            pltpu.VMEM((GDN_STEP_CHUNKS, c, GDN_CONV_DIM), F32),
            pltpu.VMEM((GDN_STEP_CHUNKS, 2, c, LANES), F32),
        ],
        compiler_params=_params("parallel", "arbitrary"),
        name="gdn",
    )(proj, proj, proj, proj, gab, conv_w, jnp.asarray(shift, BF16), pad(a_log), pad(dt_bias), gdn_gn.reshape(1, -1))


def _mix_out_kernel(h_ref, oret_ref, ogdn_ref, mgr_ref, mgg_ref, wbr_ref, wbg_ref, wo_ref, out_ref):
    y_ret = _dot(oret_ref[...], wbr_ref[...])
    y_gdn = _dot(ogdn_ref[...], wbg_ref[...])
    merged = (jax.nn.sigmoid(mgr_ref[...].astype(F32)) * y_ret
              + jax.nn.sigmoid(mgg_ref[...].astype(F32)) * y_gdn)
    out_ref[...] = h_ref[...] + _dot(merged.astype(BF16), wo_ref[...])


def _mix_out(h, o_ret, o_gdn, proj, w_br_ret, w_br_gdn, w_out, tm=512):
    n = h.shape[0]
    tile = lambda blk: pl.BlockSpec((tm, D_MODEL), lambda i: (i, blk))
    weight = pl.BlockSpec((D_MODEL, D_MODEL), lambda i: (0, 0))
    return pl.pallas_call(
        _mix_out_kernel,
        grid=(n // tm,),
        in_specs=[tile(0), tile(0), tile(0), tile(_BLK_MGR), tile(_BLK_MGG), weight, weight, weight],
        out_specs=tile(0),
        out_shape=jax.ShapeDtypeStruct((n, D_MODEL), F32),
        compiler_params=_params("parallel"),
        name="mix_out",
    )(h, o_ret, o_gdn, proj, proj, w_br_ret, w_br_gdn, w_out)


def _ple_tail(h1, p_ref, gple_ref, wple_ref, wpg_ref, gfin_ref, out_ref, final):
    ple = _dot(p_ref[0].astype(BF16), wple_ref[...])
    gate = jax.nn.sigmoid(_dot(_rms(h1, gple_ref[...]).astype(BF16), wpg_ref[...]))
    h2 = h1 + gate * ple
    out_ref[...] = _rms(h2, gfin_ref[...]) if final else h2


def _ffn_kernel(h_ref, g_ref, wg_ref, wu_ref, wd_ref, p_ref, gple_ref, wple_ref, wpg_ref, gfin_ref,
                out_ref, *, final, tf):
    h = h_ref[...]
    u = _rms(h, g_ref[...]).astype(BF16)
    for j in range(D_FF // tf):
        cols = slice(j * tf, (j + 1) * tf)
        act = _silu(_dot(u, wg_ref[:, cols])) * _dot(u, wu_ref[:, cols])
        h = h + _dot(act.astype(BF16), wd_ref[cols, :])
    _ple_tail(h, p_ref, gple_ref, wple_ref, wpg_ref, gfin_ref, out_ref, final)


def _ffn(h, g, wg, wu, wd, p, layer, g_ple, w_ple, w_ple_gate, g_final, final, tm=512, tf=256):
    n = h.shape[0]
    return pl.pallas_call(
        functools.partial(_ffn_kernel, final=final, tf=tf),
        grid=(n // tm,),
        in_specs=[
            pl.BlockSpec((tm, D_MODEL), lambda i: (i, 0)),
            _resident((1, D_MODEL)),
            _resident((D_MODEL, D_FF)),
            _resident((D_MODEL, D_FF)),
            _resident((D_FF, D_MODEL)),
            pl.BlockSpec((1, tm, PLE_DIM), lambda i: (layer, i, 0)),
            _resident((1, D_MODEL)),
            _resident((PLE_DIM, D_MODEL)),
            _resident((D_MODEL, D_MODEL)),
            _resident((1, D_MODEL)),
        ],
        out_specs=pl.BlockSpec((tm, D_MODEL), lambda i: (i, 0)),
        out_shape=jax.ShapeDtypeStruct((n, D_MODEL), F32),
        compiler_params=_params("parallel"),
        name="ffn_ple",
    )(h, g, wg, wu, wd, p, g_ple, w_ple, w_ple_gate, g_final)


def _router_kernel(h_ref, g_ref, wr_ref, route_ref):
    u = _rms(h_ref[...], g_ref[...])
    w = wr_ref[...]
    u_hi, w_hi = u.astype(BF16), w.astype(BF16)
    u_lo = (u - u_hi.astype(F32)).astype(BF16)
    w_lo = (w - w_hi.astype(F32)).astype(BF16)
    logits = _dot(u_hi, w_hi) + (_dot(u_lo, w_hi) + _dot(u_hi, w_lo))
    lane = lax.broadcasted_iota(jnp.int32, logits.shape, 1)
    logits = jnp.where(lane < N_EXPERTS, logits, -jnp.inf)
    m1 = jnp.max(logits, axis=-1, keepdims=True)
    i1 = jnp.min(jnp.where(logits == m1, lane, LANES), axis=-1, keepdims=True)
    rest = jnp.where(lane == i1, -jnp.inf, logits)
    m2 = jnp.max(rest, axis=-1, keepdims=True)
    i2 = jnp.min(jnp.where(rest == m2, lane, LANES), axis=-1, keepdims=True)
    e2 = jnp.exp(m2 - m1)
    w1 = 1.0 / (1.0 + e2)
    route_ref[...] = (jnp.where(lane == 0, i1.astype(F32), 0.0) + jnp.where(lane == 1, i2.astype(F32), 0.0)
                      + jnp.where(lane == 2, w1, 0.0) + jnp.where(lane == 3, e2 * w1, 0.0))


def _router(h, g, w_router, tm=1024):
    n = h.shape[0]
    return pl.pallas_call(
        _router_kernel,
        grid=(n // tm,),
        in_specs=[
            pl.BlockSpec((tm, D_MODEL), lambda i: (i, 0)),
            pl.BlockSpec((1, D_MODEL), lambda i: (0, 0)),
            pl.BlockSpec((D_MODEL, LANES), lambda i: (0, 0)),
        ],
        out_specs=pl.BlockSpec((tm, LANES), lambda i: (i, 0)),
        out_shape=jax.ShapeDtypeStruct((n, LANES), F32),
        compiler_params=_params("parallel"),
        name="router",
    )(h, g, w_router)


def _routing_tables(route, n):
    n_pairs = 2 * n
    fill_bit = 1 << 15
    assert n_pairs <= fill_bit and MOE_TILE <= fill_bit
    expert = route[:, 0:2].astype(jnp.int32).reshape(-1)
    ids = jnp.arange(N_EXPERTS, dtype=jnp.int32)
    counts = jnp.sum((expert[:, None] == ids[None, :]).astype(jnp.int32), axis=0)
    padded = ((counts + MOE_TILE - 1) // MOE_TILE) * MOE_TILE
    keys = (expert << 16) | jnp.arange(n_pairs, dtype=jnp.int32)
    fill_idx = jnp.arange(MOE_TILE, dtype=jnp.int32)[None, :]
    fill_keys = jnp.where(fill_idx < (padded - counts)[:, None], (ids[:, None] << 16) | fill_bit | fill_idx,
                          jnp.iinfo(jnp.int32).max)
    keys = jnp.sort(jnp.concatenate([keys, fill_keys.reshape(-1)]))
    lookahead = jnp.full(((MOE_STAGES - 1) * MOE_TILE,), jnp.iinfo(jnp.int32).max, jnp.int32)
    keys = jnp.concatenate([keys, lookahead])
    low = keys & (2 * fill_bit - 1)
    valid = low < fill_bit
    row_expert = jnp.minimum(keys >> 16, N_EXPERTS - 1)
    spare = n_pairs + row_expert * MOE_TILE + jnp.minimum(low - fill_bit, MOE_TILE - 1)
    src_row = jnp.where(valid, low >> 1, 0)
    dst_row = jnp.where(valid, (low & 1) * n + (low >> 1), spare)
    tile_expert = row_expert[::MOE_TILE]
    n_used = (jnp.sum(padded) // MOE_TILE).reshape(1)
    return src_row, dst_row, tile_expert, n_used


def _rows_start(count, src_of, dst_of, sem, unrolled):
    def one(r, priority=0):
        pltpu.make_async_copy(src_of(r), dst_of(r), sem).start(priority=priority)
    if unrolled:
        for r in range(count):
            one(r, r % 2)
    else:
        def body(r, carry):
            one(r)
            return carry
        lax.fori_loop(0, count, body, 0, unroll=8)


def _moe_gemm_kernel(src_row_ref, dst_row_ref, tile_expert_ref, n_used_ref, h_hbm, g_ref, wg_ref, wu_ref,
                     wd_ref, y_hbm, xs_ref, ys_ref, gsem, ssem):
    del tile_expert_ref
    i = pl.program_id(0)
    last = n_used_ref[0] - 1
    rows = pl.ds(0, MOE_TILE)
    slot = lambda tile: tile % MOE_STAGES

    def gather_start(tile, unrolled):
        _rows_start(MOE_TILE, lambda r: h_hbm.at[pl.ds(src_row_ref[tile * MOE_TILE + r], 1)],
                    lambda r: xs_ref.at[slot(tile), pl.ds(r, 1)], gsem.at[slot(tile)], unrolled)

    def gather_wait(tile):
        pltpu.make_async_copy(h_hbm.at[rows], xs_ref.at[slot(tile)], gsem.at[slot(tile)]).wait()

    def scatter_start(tile, unrolled):
        _rows_start(MOE_TILE, lambda r: ys_ref.at[slot(tile), pl.ds(r, 1)],
                    lambda r: y_hbm.at[pl.ds(dst_row_ref[tile * MOE_TILE + r], 1)], ssem.at[slot(tile)], unrolled)

    def scatter_wait(tile):
        pltpu.make_async_copy(ys_ref.at[slot(tile)], y_hbm.at[rows], ssem.at[slot(tile)]).wait()

    def compute(tile):
        u = _rms(xs_ref[slot(tile)], g_ref[...]).astype(BF16)
        act = _silu(_dot(u, wg_ref[0])) * _dot(u, wu_ref[0])
        ys_ref[slot(tile)] = _dot(act.astype(BF16), wd_ref[0])

    @pl.when(i == 0)
    def _():
        for tile in range(MOE_STAGES):
            gather_start(tile, False)
        ys_ref[1] = jnp.zeros((MOE_TILE, D_MODEL), F32)
        n_pairs = y_hbm.shape[0] - N_EXPERTS * MOE_TILE
        for e in range(N_EXPERTS):
            block = pltpu.make_async_copy(ys_ref.at[1], y_hbm.at[pl.ds(n_pairs + e * MOE_TILE, MOE_TILE)],
                                          ssem.at[1])
            block.start()
            block.wait()
        gather_wait(0)
        compute(0)

    @pl.when(jnp.logical_and(i >= 1, i <= last))
    def _():
        gather_wait(i)
        gather_start(i + MOE_STAGES - 1, True)
        scatter_start(i - 1, True)
        compute(i)

    @pl.when(jnp.logical_and(i >= 2, i <= last))
    def _():
        scatter_wait(i - 2)

    @pl.when(i == last)
    def _():
        for ahead in range(1, MOE_STAGES):
            gather_wait(i + ahead)

        @pl.when(i >= 1)
        def _():
            scatter_wait(i - 1)

        scatter_start(i, False)
        scatter_wait(i)


def _moe_gemm(h, g, src_row, dst_row, tile_expert, n_used, wg, wu, wd):
    n = h.shape[0]
    n_tiles = src_row.shape[0] // MOE_TILE - (MOE_STAGES - 1)
    expert_block = lambda i, sr, dr, te, nu: (te[jnp.minimum(i, nu[0] - 1)], 0, 0)
    return pl.pallas_call(
        _moe_gemm_kernel,
        grid_spec=pltpu.PrefetchScalarGridSpec(
            num_scalar_prefetch=4,
            grid=(n_tiles,),
            in_specs=[
                pl.BlockSpec(memory_space=pl.ANY),
                pl.BlockSpec((1, D_MODEL), lambda i, sr, dr, te, nu: (0, 0)),
                pl.BlockSpec((1, D_MODEL, D_FF_EXPERT), expert_block),
                pl.BlockSpec((1, D_MODEL, D_FF_EXPERT), expert_block),
                pl.BlockSpec((1, D_FF_EXPERT, D_MODEL), expert_block),
            ],
            out_specs=pl.BlockSpec(memory_space=pl.ANY),
            scratch_shapes=[
                pltpu.VMEM((MOE_STAGES, MOE_TILE, D_MODEL), F32),
                pltpu.VMEM((MOE_STAGES, MOE_TILE, D_MODEL), F32),
                pltpu.SemaphoreType.DMA((MOE_STAGES,)),
                pltpu.SemaphoreType.DMA((MOE_STAGES,)),
            ],
        ),
        out_shape=jax.ShapeDtypeStruct((2 * n + N_EXPERTS * MOE_TILE, D_MODEL), F32),
        compiler_params=_params("arbitrary"),
        name="moe_gemm",
    )(src_row, dst_row, tile_expert, n_used, h, g, wg, wu, wd)


def _moe_tail_kernel(h_ref, route_ref, y0_ref, y1_ref, p_ref, gple_ref, wple_ref, wpg_ref, gfin_ref, out_ref,
                     *, final):
    route = route_ref[...]
    h1 = h_ref[...] + route[:, 2:3] * y0_ref[...] + route[:, 3:4] * y1_ref[...]
    _ple_tail(h1, p_ref, gple_ref, wple_ref, wpg_ref, gfin_ref, out_ref, final)


def _moe_tail(h, route, y, p, layer, g_ple, w_ple, w_ple_gate, g_final, final, tm=512):
    n = h.shape[0]
    return pl.pallas_call(
        functools.partial(_moe_tail_kernel, final=final),
        grid=(n // tm,),
        in_specs=[
            pl.BlockSpec((tm, D_MODEL), lambda i: (i, 0)),
            pl.BlockSpec((tm, LANES), lambda i: (i, 0)),
            pl.BlockSpec((tm, D_MODEL), lambda i: (i, 0)),
            pl.BlockSpec((tm, D_MODEL), lambda i: (n // tm + i, 0)),
            pl.BlockSpec((1, tm, PLE_DIM), lambda i: (layer, i, 0)),
            _resident((1, D_MODEL)),
            _resident((PLE_DIM, D_MODEL)),
            _resident((D_MODEL, D_MODEL)),
            _resident((1, D_MODEL)),
        ],
        out_specs=pl.BlockSpec((tm, D_MODEL), lambda i: (i, 0)),
        out_shape=jax.ShapeDtypeStruct((n, D_MODEL), F32),
        compiler_params=_params("parallel"),
        name="moe_tail",
    )(h, route, y, y, p, g_ple, w_ple, w_ple_gate, g_final)


def _moe(h, g, w_router, wg, wu, wd, p, layer, g_ple, w_ple, w_ple_gate, g_final, final):
    n = h.shape[0]
    route = _router(h, g, w_router)
    src_row, dst_row, tile_expert, n_used = _routing_tables(route, n)
    y = _moe_gemm(h, g, src_row, dst_row, tile_expert, n_used, wg, wu, wd)
    return _moe_tail(h, route, y, p, layer, g_ple, w_ple, w_ple_gate, g_final, final)


def _rotary_tables(seq):
    half = RET_DK // 2
    pos = jnp.arange(seq, dtype=F32)
    inv_freq = ROPE_THETA ** (-jnp.arange(half, dtype=F32) / half)
    ang = pos[:, None] * inv_freq[None, :]
    cos, sin = jnp.cos(ang), jnp.sin(ang)
    return jnp.concatenate([cos, cos], axis=-1), jnp.concatenate([-sin, sin], axis=-1)


def kernel(x, p, w_in, conv_w, a_log, dt_bias, ret_gn, gdn_gn, w_br_ret, w_br_gdn, w_out, norm_mix,
           norm_ffn, norm_ple, w_ple, w_ple_gate, ffn_w_gate, ffn_w_up, ffn_w_down, router,
           exp_w_gate, exp_w_up, exp_w_down, norm_final):
    batch, seq, _ = x.shape
    depth = w_in.shape[0]
    n = batch * seq
    cos_t, sin_t = _rotary_tables(seq)
    g_final = norm_final.reshape(1, -1)
    h = x.reshape(n, D_MODEL)
    for i in range(depth):
        w_gab = jnp.pad(w_in[i, :, OFF_GA:OFF_MG], ((0, 0), (0, LANES - 2 * GDN_HEADS))).astype(BF16)
        proj, gab = _in_proj(h, norm_mix[i].reshape(1, -1), w_in[i, :, :OFF_GA].astype(BF16),
                             w_in[i, :, OFF_MG:].astype(BF16), w_gab)
        o_ret = _retention(proj, cos_t, sin_t, ret_gn[i].reshape(1, -1), batch, seq)
        o_gdn = _gdn(proj, gab, conv_w[i], a_log[i], dt_bias[i], gdn_gn[i], batch, seq)
        h = _mix_out(h, o_ret, o_gdn, proj, w_br_ret[i].astype(BF16), w_br_gdn[i].astype(BF16),
                     w_out[i].astype(BF16))
        final = i == depth - 1
        tail = (p.reshape(depth, n, PLE_DIM), i, norm_ple[i].reshape(1, -1), w_ple[i].astype(BF16),
                w_ple_gate[i].astype(BF16), g_final, final)
        j = i // 2
        g_ffn = norm_ffn[i].reshape(1, -1)
        if i % 2 == 0:
            h = _ffn(h, g_ffn, ffn_w_gate[j].astype(BF16), ffn_w_up[j].astype(BF16),
                     ffn_w_down[j].astype(BF16), *tail)
        else:
            w_router = jnp.pad(router[j], ((0, 0), (0, LANES - N_EXPERTS)))
            h = _moe(h, g_ffn, w_router, exp_w_gate[j].astype(BF16), exp_w_up[j].astype(BF16),
                     exp_w_down[j].astype(BF16), *tail)
    return h.reshape(batch, seq, D_MODEL)
```
